```python
import jax, jax.numpy as jnp
from jax import lax
import numpy as np

D_MODEL = 1024
BATCH = 4
SEQ = 4096
DEPTH = 2

CHUNK = 128
RET_HEADS = 4
RET_HEAD_DIM = 128
RET_W = RET_HEADS * RET_HEAD_DIM
SB_HEADS = 8
SB_HEAD_DIM = 64
SB_W = SB_HEADS * SB_HEAD_DIM
SGU_GROUPS = 4
SGU_GROUP_DIM = 128
SGU_W = SGU_GROUPS * SGU_GROUP_DIM
D_FF = 4 * D_MODEL
ROPE_BASE = 10000.0
LN_EPS = 1e-5
DEEPNORM_ALPHA = (2 * DEPTH) ** 0.25
DEEPNORM_BETA = (8 * DEPTH) ** -0.25
SPLITS = (RET_W, RET_W, RET_W, RET_W, SB_W, SB_W, SB_W, SGU_W, SGU_W, D_MODEL, D_MODEL, D_MODEL)
N_IN = sum(SPLITS)

kernel_name = "hybrid_ret_sb_sgu_deepnorm"


def layer_norm(x, g, b):
    xf = x.astype(jnp.float32)
    mu = jnp.mean(xf, axis=-1, keepdims=True)
    var = jnp.mean(jnp.square(xf - mu), axis=-1, keepdims=True)
    y = (xf - mu) * lax.rsqrt(var + LN_EPS)
    return (y * g.astype(jnp.float32) + b.astype(jnp.float32)).astype(x.dtype)


def head_group_norm(o, g, b):
    B, S, H, D = o.shape
    of = o.astype(jnp.float32)
    mu = jnp.mean(of, axis=-1, keepdims=True)
    var = jnp.mean(jnp.square(of - mu), axis=-1, keepdims=True)
    y = ((of - mu) * lax.rsqrt(var + LN_EPS)).reshape(B, S, H * D)
    return (y * g.astype(jnp.float32) + b.astype(jnp.float32)).astype(o.dtype)


def rotary(x, pos):
    half = x.shape[-1] // 2
    inv_freq = ROPE_BASE ** (-jnp.arange(half, dtype=jnp.float32) / half)
    ang = pos.astype(jnp.float32)[:, None] * inv_freq[None, :]
    cos = jnp.cos(ang)[None, :, None, :].astype(x.dtype)
    sin = jnp.sin(ang)[None, :, None, :].astype(x.dtype)
    x1, x2 = x[..., :half], x[..., half:]
    return jnp.concatenate([x1 * cos - x2 * sin, x2 * cos + x1 * sin], axis=-1)


def retention(q, k, v):
    B, S, H, D = q.shape
    N = S // CHUNK
    dt = q.dtype
    log_g = jnp.log(1.0 - 2.0 ** (-5.0 - jnp.arange(H, dtype=jnp.float32)))
    idx = jnp.arange(CHUNK, dtype=jnp.float32)
    diff = idx[:, None] - idx[None, :]
    intra_decay = jnp.where(diff[None] >= 0, jnp.exp(log_g[:, None, None] * diff[None]), 0.0).astype(dt)
    k_decay = jnp.exp(log_g[:, None] * (CHUNK - 1 - idx)[None, :]).T.astype(dt)
    q_decay = jnp.exp(log_g[:, None] * (idx + 1.0)[None, :]).T.astype(dt)
    chunk_decay = jnp.exp(log_g * CHUNK).astype(dt)

    qc = q.reshape(B, N, CHUNK, H, D)
    kc = k.reshape(B, N, CHUNK, H, D)
    vc = v.reshape(B, N, CHUNK, H, D)

    scores = jnp.einsum('bnihd,bnjhd->bnhij', qc, kc) * intra_decay
    intra = jnp.einsum('bnhij,bnjhe->bnihe', scores, vc)

    kv = jnp.einsum('bnjhd,bnjhe->nbhde', kc * k_decay[:, :, None], vc)

    def step(state, kv_n):
        new_state = state * chunk_decay[None, :, None, None] + kv_n
        return new_state, state

    state0 = jnp.zeros((B, H, D, D), dt)
    _, prev_states = lax.scan(step, state0, kv)
    inter = jnp.einsum('bnihd,nbhde->bnihe', qc * q_decay[:, :, None], prev_states)
    return (intra + inter).reshape(B, S, H, D)


def stick_breaking(q, k, v):
    B, S, H, D = q.shape
    NB = S // CHUNK
    scale = D ** -0.5
    qb = q.reshape(B, NB, CHUNK, H, D).transpose(1, 0, 2, 3, 4)
    s_pos = jnp.arange(S)

    def block(args):
        qn, n = args
        z = jnp.einsum('bihd,bshd->bhis', qn, k).astype(jnp.float32) * scale
        t_pos = n * CHUNK + jnp.arange(CHUNK)
        mask = s_pos[None, :] < t_pos[:, None]
        log_1m_beta = jnp.where(mask, jax.nn.log_sigmoid(-z), 0.0)
        later = lax.cumsum(log_1m_beta, axis=3, reverse=True) - log_1m_beta
        a = jnp.where(mask, jnp.exp(jax.nn.log_sigmoid(z) + later), 0.0)
        return jnp.einsum('bhis,bshd->bihd', a.astype(v.dtype), v)

    out = lax.map(block, (qb, jnp.arange(NB)))
    return out.transpose(1, 0, 2, 3, 4).reshape(B, S, H * D)


def chunked_sgu(u, v, ln_g, ln_b, w_s, b_s):
    B, S, _ = v.shape
    N = S // CHUNK
    v = layer_norm(v, ln_g, ln_b)
    vg = v.reshape(B, N, CHUNK, SGU_GROUPS, SGU_GROUP_DIM)
    causal = jnp.tril(jnp.ones((CHUNK, CHUNK), dtype=w_s.dtype))
    w = w_s * causal[None]
    sv = jnp.einsum('gij,bnjgc->bnigc', w, vg) + b_s.T[None, None, :, :, None]
    return u * sv.reshape(B, S, SGU_W)


def mixer(x, w_in, ret_gn_g, ret_gn_b, sgu_ln_g, sgu_ln_b, sgu_w, sgu_b, p_ret, p_sb, p_sgu, w_out):
    B, S, _ = x.shape
    proj = x @ w_in
    points = [int(p) for p in np.cumsum(SPLITS)[:-1]]
    (rq, rk, rv, rg, sq, sk, sv, gu, gv, gate_ret, gate_sb, gate_sgu) = jnp.split(proj, points, axis=-1)
    pos = jnp.arange(S, dtype=jnp.int32)

    rq = rotary(rq.reshape(B, S, RET_HEADS, RET_HEAD_DIM), pos)
    rk = rotary(rk.reshape(B, S, RET_HEADS, RET_HEAD_DIM), pos) * (RET_HEAD_DIM ** -0.5)
    ret = retention(rq, rk, rv.reshape(B, S, RET_HEADS, RET_HEAD_DIM))
    ret = jax.nn.silu(rg) * head_group_norm(ret, ret_gn_g, ret_gn_b)

    sb = stick_breaking(sq.reshape(B, S, SB_HEADS, SB_HEAD_DIM),
                        sk.reshape(B, S, SB_HEADS, SB_HEAD_DIM),
                        sv.reshape(B, S, SB_HEADS, SB_HEAD_DIM))

    sg = chunked_sgu(jax.nn.gelu(gu), jax.nn.gelu(gv), sgu_ln_g, sgu_ln_b, sgu_w, sgu_b)

    merged = (jax.nn.sigmoid(gate_ret) * (ret @ p_ret)
              + jax.nn.sigmoid(gate_sb) * (sb @ p_sb)
              + jax.nn.sigmoid(gate_sgu) * (sg @ p_sgu))
    return merged @ w_out


def setup_inputs(seed: int = 0) -> dict:
    key = jax.random.key(seed)
    ks = jax.random.split(key, 20)
    L = DEPTH
    f32 = jnp.float32

    def nrm(k, shape, scale):
        return jax.random.normal(k, shape, f32) * scale

    return {
        "x": jax.random.normal(ks[0], (BATCH, SEQ, D_MODEL), f32),
        "w_in": nrm(ks[1], (L, D_MODEL, N_IN), D_MODEL ** -0.5),
        "ret_gn_g": 1.0 + nrm(ks[2], (L, RET_W), 0.02),
        "ret_gn_b": nrm(ks[3], (L, RET_W), 0.02),
        "sgu_ln_g": 1.0 + nrm(ks[4], (L, SGU_W), 0.02),
        "sgu_ln_b": nrm(ks[5], (L, SGU_W), 0.02),
        "sgu_w": nrm(ks[6], (L, SGU_GROUPS, CHUNK, CHUNK), CHUNK ** -0.5),
        "sgu_b": 1.0 + nrm(ks[7], (L, SGU_GROUPS, CHUNK), 0.01),
        "p_ret": nrm(ks[8], (L, RET_W, D_MODEL), RET_W ** -0.5 * DEEPNORM_BETA),
        "p_sb": nrm(ks[9], (L, SB_W, D_MODEL), SB_W ** -0.5 * DEEPNORM_BETA),
        "p_sgu": nrm(ks[10], (L, SGU_W, D_MODEL), SGU_W ** -0.5 * DEEPNORM_BETA),
        "w_out": nrm(ks[11], (L, D_MODEL, D_MODEL), D_MODEL ** -0.5 * DEEPNORM_BETA),
        "ln1_g": 1.0 + nrm(ks[12], (L, D_MODEL), 0.02),
        "ln1_b": nrm(ks[13], (L, D_MODEL), 0.02),
        "w_up": nrm(ks[14], (L, D_MODEL, D_FF), D_MODEL ** -0.5 * DEEPNORM_BETA),
        "w_down": nrm(ks[15], (L, D_FF, D_MODEL), D_FF ** -0.5 * DEEPNORM_BETA),
        "ln2_g": 1.0 + nrm(ks[16], (L, D_MODEL), 0.02),
        "ln2_b": nrm(ks[17], (L, D_MODEL), 0.02),
    }


def reference(x, w_in, ret_gn_g, ret_gn_b, sgu_ln_g, sgu_ln_b, sgu_w, sgu_b, p_ret, p_sb, p_sgu,
              w_out, ln1_g, ln1_b, w_up, w_down, ln2_g, ln2_b):
    for l in range(DEPTH):
        y = mixer(x, w_in[l], ret_gn_g[l], ret_gn_b[l], sgu_ln_g[l], sgu_ln_b[l], sgu_w[l], sgu_b[l],
                  p_ret[l], p_sb[l], p_sgu[l], w_out[l])
        x = layer_norm(DEEPNORM_ALPHA * x + y, ln1_g[l], ln1_b[l])
        h = jnp.square(jax.nn.relu(x @ w_up[l])) @ w_down[l]
        x = layer_norm(DEEPNORM_ALPHA * x + h, ln2_g[l], ln2_b[l])
    return x
```

```python
import functools

import jax
import jax.numpy as jnp
from jax import lax
from jax.experimental import pallas as pl
from jax.experimental.pallas import tpu as pltpu

F32 = jnp.float32
BF16 = jnp.bfloat16

D_MODEL = 1024
DEPTH = 2
CHUNK = 128
RET_HEADS = 4
RET_HEAD_DIM = 128
SB_HEADS = 8
SB_HEAD_DIM = 64
SGU_GROUPS = 4
SEG = 512
D_FF = 4 * D_MODEL
N_IN = 9 * SEG + 3 * D_MODEL
ROPE_BASE = 10000.0
LN_EPS = 1e-5
DEEPNORM_ALPHA = (2 * DEPTH) ** 0.25

SEG_RQ, SEG_RK, SEG_RV, SEG_RG, SEG_SQ, SEG_SK, SEG_SV, SEG_GU, SEG_GV = range(9)
SEG_GATE0 = 9
N_SEG = N_IN // SEG

LANES = 128
VMEM_LIMIT = 56 * 1024 * 1024


def _cparams(n_axes):
    return pltpu.CompilerParams(
        dimension_semantics=("arbitrary",) * n_axes,
        vmem_limit_bytes=VMEM_LIMIT)


def _resident(shape):
    zeros = (0,) * len(shape)
    return pl.BlockSpec(shape, lambda *_: zeros, pipeline_mode=pl.Buffered(1))


def _layer_norm(x, g, b):
    mu = jnp.mean(x, axis=-1, keepdims=True)
    xc = x - mu
    var = jnp.mean(xc * xc, axis=-1, keepdims=True)
    return xc * lax.rsqrt(var + LN_EPS) * g + b


def _inproj_kernel(x_ref, w_ref, cs_ref, sn_ref, lng_ref, lnb_ref, o_ref):
    xb = x_ref[...].astype(BF16)
    cs = cs_ref[...]
    sn = sn_ref[...]
    for j in range(N_SEG):
        cols = slice(j * SEG, (j + 1) * SEG)
        acc = jnp.dot(xb, w_ref[:, cols], preferred_element_type=F32)
        if j in (SEG_RQ, SEG_RK):
            scale = RET_HEAD_DIM ** -0.5 if j == SEG_RK else None
            for h in range(RET_HEADS):
                hc = slice(h * RET_HEAD_DIM, (h + 1) * RET_HEAD_DIM)
                xh = acc[:, hc]
                r = xh * cs + pltpu.roll(xh, RET_HEAD_DIM // 2, axis=1) * sn
                if scale is not None:
                    r = r * scale
                o_ref[:, j * SEG + h * RET_HEAD_DIM:j * SEG + (h + 1) * RET_HEAD_DIM] = r.astype(BF16)
            continue
        if j == SEG_RG:
            res = jax.nn.silu(acc)
        elif j == SEG_SQ:
            res = acc * (SB_HEAD_DIM ** -0.5)
        elif j == SEG_GU:
            res = jax.nn.gelu(acc)
        elif j == SEG_GV:
            res = _layer_norm(jax.nn.gelu(acc), lng_ref[...], lnb_ref[...])
        elif j >= SEG_GATE0:
            res = jax.nn.sigmoid(acc)
        else:
            res = acc
        o_ref[:, cols] = res.astype(BF16)


def _in_proj(x2, w, cs, sn, lng, lnb, tm, seq):
    t = x2.shape[0]
    pos_blocks = seq // tm
    return pl.pallas_call(
        _inproj_kernel,
        grid=(t // tm,),
        in_specs=[
            pl.BlockSpec((tm, D_MODEL), lambda i: (i, 0)),
            _resident((D_MODEL, N_IN)),
            pl.BlockSpec((tm, LANES), lambda i: (i % pos_blocks, 0)),
            pl.BlockSpec((tm, LANES), lambda i: (i % pos_blocks, 0)),
            _resident((1, SEG)),
            _resident((1, SEG)),
        ],
        out_specs=pl.BlockSpec((tm, N_IN), lambda i: (i, 0)),
        out_shape=jax.ShapeDtypeStruct((t, N_IN), BF16),
        compiler_params=_cparams(1),
        name="in_proj",
    )(x2, w, cs, sn, lng, lnb)


def _ret_kernel(q_ref, k_ref, v_ref, g_ref, dint_ref, qd_ref, kd_ref, cd_ref,
                gng_ref, gnb_ref, o_ref, state_ref):
    n_chunks = q_ref.shape[0] // CHUNK
    state_ref[...] = jnp.zeros_like(state_ref)
    dint = dint_ref[0]
    qd = qd_ref[0]
    kd = kd_ref[0]
    cd = cd_ref[0]
    gng = gng_ref[...]
    gnb = gnb_ref[...]

    def body(n, carry):
        rows = pl.ds(pl.multiple_of(n * CHUNK, CHUNK), CHUNK)
        q = q_ref[rows, :]
        k = k_ref[rows, :]
        v = v_ref[rows, :]
        s = lax.dot_general(q, k, (((1,), (1,)), ((), ())), preferred_element_type=F32) * dint
        intra = jnp.dot(s.astype(BF16), v, preferred_element_type=F32)
        st = state_ref[...]
        qdec = (q.astype(F32) * qd).astype(BF16)
        inter = jnp.dot(qdec, st.astype(BF16), preferred_element_type=F32)
        kdec = (k.astype(F32) * kd).astype(BF16)
        kv = lax.dot_general(kdec, v, (((0,), (0,)), ((), ())), preferred_element_type=F32)
        state_ref[...] = st * cd + kv
        y = _layer_norm(intra + inter, gng, gnb)
        o_ref[rows, :] = (g_ref[rows, :].astype(F32) * y).astype(BF16)
        return carry

    lax.fori_loop(0, n_chunks, body, 0)


def _retention(proj, tabs, gng, gnb, batch, seq):
    dint, qd, kd, cd = tabs
    t = proj.shape[0]
    hb = SEG // RET_HEAD_DIM

    def seg_spec(seg):
        return pl.BlockSpec((seq, RET_HEAD_DIM), lambda b, h: (b, seg * hb + h))

    tab_spec = pl.BlockSpec((1, CHUNK, RET_HEAD_DIM), lambda b, h: (h, 0, 0))
    return pl.pallas_call(
        _ret_kernel,
        grid=(batch, RET_HEADS),
        in_specs=[seg_spec(SEG_RQ), seg_spec(SEG_RK), seg_spec(SEG_RV), seg_spec(SEG_RG),
                  tab_spec, tab_spec, tab_spec, tab_spec,
                  pl.BlockSpec((1, RET_HEAD_DIM), lambda b, h: (0, h)),
                  pl.BlockSpec((1, RET_HEAD_DIM), lambda b, h: (0, h))],
        out_specs=pl.BlockSpec((seq, RET_HEAD_DIM), lambda b, h: (b, h)),
        out_shape=jax.ShapeDtypeStruct((t, SEG), BF16),
        scratch_shapes=[pltpu.VMEM((RET_HEAD_DIM, RET_HEAD_DIM), F32)],
        compiler_params=_cparams(2),
        name="retention",
    )(proj, proj, proj, proj, dint, qd, kd, cd, gng, gnb)


def _log_sigmoid_neg(z):
    return -(jnp.maximum(z, 0.0) + jnp.log1p(jnp.exp(-jnp.abs(z))))


def _sb_kernel(q_ref, k_ref, v_ref, o_ref):
    qi = pl.program_id(2)
    q = q_ref[...]
    lane = lax.broadcasted_iota(jnp.int32, (CHUNK, LANES), 1)
    row = lax.broadcasted_iota(jnp.int32, (CHUNK, CHUNK), 0)
    col = lax.broadcasted_iota(jnp.int32, (CHUNK, CHUNK), 1)
    low = lane < SB_HEAD_DIM
    zero = jnp.zeros_like(q)
    q_heads = (jnp.where(low, q, zero), jnp.where(low, zero, q))
    tri = jnp.where(row > col, 1.0, 0.0).astype(BF16)
    t_pos = qi * CHUNK + row

    def body(i, carry):
        kj = qi - i
        rows = pl.ds(pl.multiple_of(kj * CHUNK, CHUNK), CHUNK)
        kb = k_ref[rows, :]
        vb = v_ref[rows, :]
        mask = (kj * CHUNK + col) < t_pos
        new = []
        for hh in range(2):
            acc, run = carry[hh]
            z = lax.dot_general(q_heads[hh], kb, (((1,), (1,)), ((), ())),
                                preferred_element_type=F32)
            lsn = _log_sigmoid_neg(z)
            l = jnp.where(mask, lsn, 0.0)
            l_hi = l.astype(BF16)
            l_lo = (l - l_hi.astype(F32)).astype(BF16)
            later = (jnp.dot(l_hi, tri, preferred_element_type=F32)
                     + jnp.dot(l_lo, tri, preferred_element_type=F32)) + run
            a = jnp.where(mask, jnp.exp(lsn + z + later), 0.0)
            acc = acc + jnp.dot(a.astype(BF16), vb, preferred_element_type=F32)
            run = run + jnp.sum(l, axis=1, keepdims=True)
            new.append((acc, run))
        return tuple(new)

    init = tuple((jnp.zeros((CHUNK, LANES), F32), jnp.zeros((CHUNK, CHUNK), F32))
                 for _ in range(2))
    (acc0, _), (acc1, _) = lax.fori_loop(0, qi + 1, body, init)
    o_ref[...] = jnp.where(low, acc0, acc1).astype(BF16)


def _stick_breaking(proj, batch, seq):
    t = proj.shape[0]
    nq = seq // CHUNK
    pairs = SB_HEADS * SB_HEAD_DIM // LANES
    cb = SEG // LANES
    return pl.pallas_call(
        _sb_kernel,
        grid=(batch, pairs, nq),
        in_specs=[
            pl.BlockSpec((CHUNK, LANES), lambda b, p, i: (b * nq + i, SEG_SQ * cb + p)),
            pl.BlockSpec((seq, LANES), lambda b, p, i: (b, SEG_SK * cb + p)),
            pl.BlockSpec((seq, LANES), lambda b, p, i: (b, SEG_SV * cb + p)),
        ],
        out_specs=pl.BlockSpec((CHUNK, LANES), lambda b, p, i: (b * nq + i, p)),
        out_shape=jax.ShapeDtypeStruct((t, SEG), BF16),
        compiler_params=_cparams(3),
        name="stick_breaking",
    )(proj, proj, proj)


def _sgu_kernel(u_ref, v_ref, w_ref, b_ref, o_ref):
    row = lax.broadcasted_iota(jnp.int32, (CHUNK, CHUNK), 0)
    col = lax.broadcasted_iota(jnp.int32, (CHUNK, CHUNK), 1)
    causal = row >= col
    n_chunks = u_ref.shape[0] // CHUNK
    for g in range(SGU_GROUPS):
        wm = jnp.where(causal, w_ref[g], 0.0).astype(BF16)
        bias = b_ref[g]
        gc = slice(g * LANES, (g + 1) * LANES)
        for c in range(n_chunks):
            rc = slice(c * CHUNK, (c + 1) * CHUNK)
            sv = jnp.dot(wm, v_ref[rc, gc], preferred_element_type=F32) + bias
            o_ref[rc, gc] = (u_ref[rc, gc].astype(F32) * sv).astype(BF16)


def _sgu(proj, w_s, b_tab, tm):
    t = proj.shape[0]
    return pl.pallas_call(
        _sgu_kernel,
        grid=(t // tm,),
        in_specs=[
            pl.BlockSpec((tm, SEG), lambda i: (i, SEG_GU)),
            pl.BlockSpec((tm, SEG), lambda i: (i, SEG_GV)),
            _resident((SGU_GROUPS, CHUNK, CHUNK)),
            _resident((SGU_GROUPS, CHUNK, LANES)),
        ],
        out_specs=pl.BlockSpec((tm, SEG), lambda i: (i, 0)),
        out_shape=jax.ShapeDtypeStruct((t, SEG), BF16),
        compiler_params=_cparams(1),
        name="sgu",
    )(proj, proj, w_s, b_tab)


def _merge_kernel(x_ref, ret_ref, sb_ref, sg_ref, g0, g1, g2, g3, g4, g5,
                  pr_ref, ps_ref, pg_ref, wo_ref, lng_ref, lnb_ref, o_ref):
    gates = ((g0, g1), (g2, g3), (g4, g5))
    branches = ((ret_ref, pr_ref), (sb_ref, ps_ref), (sg_ref, pg_ref))
    halves = []
    for half in range(D_MODEL // SEG):
        cols = slice(half * SEG, (half + 1) * SEG)
        m = None
        for (a_ref, p_ref), gate in zip(branches, gates):
            term = gate[half][...].astype(F32) * jnp.dot(
                a_ref[...], p_ref[:, cols], preferred_element_type=F32)
            m = term if m is None else m + term
        halves.append(m.astype(BF16))
    y = (jnp.dot(halves[0], wo_ref[:SEG, :], preferred_element_type=F32)
         + jnp.dot(halves[1], wo_ref[SEG:, :], preferred_element_type=F32))
    o_ref[...] = _layer_norm(DEEPNORM_ALPHA * x_ref[...] + y, lng_ref[...], lnb_ref[...])


def _merge(x2, ret, sb, sg, proj, p_ret, p_sb, p_sgu, w_out, lng, lnb, tm):
    t = x2.shape[0]
    row_spec = lambda w: pl.BlockSpec((tm, w), lambda i: (i, 0))
    gate_specs = [pl.BlockSpec((tm, SEG), functools.partial(lambda i, j: (i, j), j=SEG_GATE0 + j))
                  for j in range(6)]
    return pl.pallas_call(
        _merge_kernel,
        grid=(t // tm,),
        in_specs=[row_spec(D_MODEL), row_spec(SEG), row_spec(SEG), row_spec(SEG)]
        + gate_specs
        + [_resident((SEG, D_MODEL))] * 3
        + [_resident((D_MODEL, D_MODEL)), _resident((1, D_MODEL)), _resident((1, D_MODEL))],
        out_specs=row_spec(D_MODEL),
        out_shape=jax.ShapeDtypeStruct((t, D_MODEL), F32),
        compiler_params=_cparams(1),
        name="merge_out",
    )(x2, ret, sb, sg, *([proj] * 6), p_ret, p_sb, p_sgu, w_out, lng, lnb)


FF_CHUNK = 1024


def _mlp_kernel(x_ref, wu_ref, wd_ref, lng_ref, lnb_ref, o_ref):
    x = x_ref[...]
    xb = x.astype(BF16)
    acc = None
    for c in range(D_FF // FF_CHUNK):
        cols = slice(c * FF_CHUNK, (c + 1) * FF_CHUNK)
        h = jnp.maximum(jnp.dot(xb, wu_ref[:, cols], preferred_element_type=F32), 0.0)
        part = jnp.dot((h * h).astype(BF16), wd_ref[cols, :], preferred_element_type=F32)
        acc = part if acc is None else acc + part
    o_ref[...] = _layer_norm(DEEPNORM_ALPHA * x + acc, lng_ref[...], lnb_ref[...])


def _mlp(x2, w_up, w_down, lng, lnb, tm):
    t = x2.shape[0]
    return pl.pallas_call(
        _mlp_kernel,
        grid=(t // tm,),
        in_specs=[pl.BlockSpec((tm, D_MODEL), lambda i: (i, 0)),
                  _resident((D_MODEL, D_FF)), _resident((D_FF, D_MODEL)),
                  _resident((1, D_MODEL)), _resident((1, D_MODEL))],
        out_specs=pl.BlockSpec((tm, D_MODEL), lambda i: (i, 0)),
        out_shape=jax.ShapeDtypeStruct((t, D_MODEL), F32),
        compiler_params=_cparams(1),
        name="mlp",
    )(x2, w_up, w_down, lng, lnb)


def _rotary_tables(seq):
    half = RET_HEAD_DIM // 2
    inv_freq = ROPE_BASE ** (-jnp.arange(half, dtype=F32) / half)
    ang = jnp.arange(seq, dtype=jnp.int32).astype(F32)[:, None] * inv_freq[None, :]
    cos, sin = jnp.cos(ang), jnp.sin(ang)
    return jnp.concatenate([cos, cos], axis=1), jnp.concatenate([-sin, sin], axis=1)


def _retention_tables():
    log_g = jnp.log(1.0 - 2.0 ** (-5.0 - jnp.arange(RET_HEADS, dtype=F32)))
    idx = jnp.arange(CHUNK, dtype=F32)
    diff = idx[:, None] - idx[None, :]
    dint = jnp.where(diff[None] >= 0, jnp.exp(log_g[:, None, None] * diff[None]), 0.0)
    k_decay = jnp.exp(log_g[:, None] * (CHUNK - 1 - idx)[None, :])
    q_decay = jnp.exp(log_g[:, None] * (idx + 1.0)[None, :])
    chunk_decay = jnp.exp(log_g * CHUNK)
    full = (RET_HEADS, CHUNK, RET_HEAD_DIM)
    return (dint.astype(F32),
            jnp.broadcast_to(q_decay[:, :, None], full),
            jnp.broadcast_to(k_decay[:, :, None], full),
            jnp.broadcast_to(chunk_decay[:, None, None], full))


def kernel(x, w_in, ret_gn_g, ret_gn_b, sgu_ln_g, sgu_ln_b, sgu_w, sgu_b, p_ret, p_sb, p_sgu,
           w_out, ln1_g, ln1_b, w_up, w_down, ln2_g, ln2_b):
    batch, seq, d = x.shape
    assert d == D_MODEL and seq % CHUNK == 0
    t = batch * seq
    tm = 512
    assert seq % tm == 0
    cs, sn = _rotary_tables(seq)
    ret_tabs = _retention_tables()
    row = lambda a: a.reshape(1, -1)
    x2 = x.reshape(t, d)
    for l in range(DEPTH):
        proj = _in_proj(x2, w_in[l].astype(BF16), cs, sn, row(sgu_ln_g[l]), row(sgu_ln_b[l]),
                        tm, seq)
        ret = _retention(proj, ret_tabs, row(ret_gn_g[l]), row(ret_gn_b[l]), batch, seq)
        sb = _stick_breaking(proj, batch, seq)
        b_tab = jnp.broadcast_to(sgu_b[l][:, :, None], (SGU_GROUPS, CHUNK, LANES))
        sg = _sgu(proj, sgu_w[l], b_tab, tm)
        x2 = _merge(x2, ret, sb, sg, proj, p_ret[l].astype(BF16), p_sb[l].astype(BF16),
                    p_sgu[l].astype(BF16), w_out[l].astype(BF16), row(ln1_g[l]), row(ln1_b[l]), tm)
        x2 = _mlp(x2, w_up[l].astype(BF16), w_down[l].astype(BF16), row(ln2_g[l]), row(ln2_b[l]), tm)
    return x2.reshape(batch, seq, d)
```

```python
import functools

import jax
import jax.numpy as jnp
from jax import lax
from jax.experimental import pallas as pl
from jax.experimental.pallas import tpu as pltpu

F32 = jnp.float32
BF16 = jnp.bfloat16

D_MODEL = 1024
DEPTH = 2
CHUNK = 128
RET_HEADS = 4
RET_HEAD_DIM = 128
SB_HEADS = 8
SB_HEAD_DIM = 64
SGU_GROUPS = 4
SEG = 512
D_FF = 4 * D_MODEL
N_IN = 9 * SEG + 3 * D_MODEL
ROPE_BASE = 10000.0
LN_EPS = 1e-5
DEEPNORM_ALPHA = (2 * DEPTH) ** 0.25

SEG_RQ, SEG_RK, SEG_RV, SEG_RG, SEG_SQ, SEG_SK, SEG_SV, SEG_GU, SEG_GV = range(9)
SEG_GATE0 = 9
N_SEG = N_IN // SEG

LANES = 128
VMEM_LIMIT = 56 * 1024 * 1024


def _cparams(n_axes):
    return pltpu.CompilerParams(
        dimension_semantics=("arbitrary",) * n_axes,
        vmem_limit_bytes=VMEM_LIMIT)


def _resident(shape):
    zeros = (0,) * len(shape)
    return pl.BlockSpec(shape, lambda *_: zeros, pipeline_mode=pl.Buffered(1))


def _layer_norm(x, g, b):
    mu = jnp.mean(x, axis=-1, keepdims=True)
    xc = x - mu
    var = jnp.mean(xc * xc, axis=-1, keepdims=True)
    return xc * lax.rsqrt(var + LN_EPS) * g + b


def _inproj_kernel(x_ref, w_ref, cs_ref, sn_ref, lng_ref, lnb_ref, o_ref):
    xb = x_ref[...].astype(BF16)
    cs = cs_ref[...]
    sn = sn_ref[...]
    for j in range(N_SEG):
        cols = slice(j * SEG, (j + 1) * SEG)
        acc = jnp.dot(xb, w_ref[:, cols], preferred_element_type=F32)
        if j in (SEG_RQ, SEG_RK):
            scale = RET_HEAD_DIM ** -0.5 if j == SEG_RK else None
            for h in range(RET_HEADS):
                hc = slice(h * RET_HEAD_DIM, (h + 1) * RET_HEAD_DIM)
                xh = acc[:, hc]
                r = xh * cs + pltpu.roll(xh, RET_HEAD_DIM // 2, axis=1) * sn
                if scale is not None:
                    r = r * scale
                o_ref[:, j * SEG + h * RET_HEAD_DIM:j * SEG + (h + 1) * RET_HEAD_DIM] = r.astype(BF16)
            continue
        if j == SEG_RG:
            res = jax.nn.silu(acc)
        elif j == SEG_SQ:
            res = acc * (SB_HEAD_DIM ** -0.5)
        elif j == SEG_GU:
            res = jax.nn.gelu(acc)
        elif j == SEG_GV:
            res = _layer_norm(jax.nn.gelu(acc), lng_ref[...], lnb_ref[...])
        elif j >= SEG_GATE0:
            res = jax.nn.sigmoid(acc)
        else:
            res = acc
        o_ref[:, cols] = res.astype(BF16)


def _in_proj(x2, w, cs, sn, lng, lnb, tm, seq):
    t = x2.shape[0]
    pos_blocks = seq // tm
    return pl.pallas_call(
        _inproj_kernel,
        grid=(t // tm,),
        in_specs=[
            pl.BlockSpec((tm, D_MODEL), lambda i: (i, 0)),
            _resident((D_MODEL, N_IN)),
            pl.BlockSpec((tm, LANES), lambda i: (i % pos_blocks, 0)),
            pl.BlockSpec((tm, LANES), lambda i: (i % pos_blocks, 0)),
            _resident((1, SEG)),
            _resident((1, SEG)),
        ],
        out_specs=pl.BlockSpec((tm, N_IN), lambda i: (i, 0)),
        out_shape=jax.ShapeDtypeStruct((t, N_IN), BF16),
        compiler_params=_cparams(1),
        name="in_proj",
    )(x2, w, cs, sn, lng, lnb)


def _ret_kernel(q_ref, k_ref, v_ref, g_ref, dint_ref, qd_ref, kd_ref, cd_ref,
                gng_ref, gnb_ref, o_ref, state_ref):
    n_chunks = q_ref.shape[0] // CHUNK
    state_ref[...] = jnp.zeros_like(state_ref)
    dint = dint_ref[0]
    qd = qd_ref[0]
    kd = kd_ref[0]
    cd = cd_ref[0]
    gng = gng_ref[...]
    gnb = gnb_ref[...]

    def body(n, carry):
        rows = pl.ds(pl.multiple_of(n * CHUNK, CHUNK), CHUNK)
        q = q_ref[rows, :]
        k = k_ref[rows, :]
        v = v_ref[rows, :]
        s = lax.dot_general(q, k, (((1,), (1,)), ((), ())), preferred_element_type=F32) * dint
        intra = jnp.dot(s.astype(BF16), v, preferred_element_type=F32)
        st = state_ref[...]
        qdec = (q.astype(F32) * qd).astype(BF16)
        inter = jnp.dot(qdec, st.astype(BF16), preferred_element_type=F32)
        kdec = (k.astype(F32) * kd).astype(BF16)
        kv = lax.dot_general(kdec, v, (((0,), (0,)), ((), ())), preferred_element_type=F32)
        state_ref[...] = st * cd + kv
        y = _layer_norm(intra + inter, gng, gnb)
        o_ref[rows, :] = (g_ref[rows, :].astype(F32) * y).astype(BF16)
        return carry

    lax.fori_loop(0, n_chunks, body, 0)


def _retention(proj, tabs, gng, gnb, batch, seq):
    dint, qd, kd, cd = tabs
    t = proj.shape[0]
    hb = SEG // RET_HEAD_DIM

    def seg_spec(seg):
        return pl.BlockSpec((seq, RET_HEAD_DIM), lambda b, h: (b, seg * hb + h))

    tab_spec = pl.BlockSpec((1, CHUNK, RET_HEAD_DIM), lambda b, h: (h, 0, 0))
    return pl.pallas_call(
        _ret_kernel,
        grid=(batch, RET_HEADS),
        in_specs=[seg_spec(SEG_RQ), seg_spec(SEG_RK), seg_spec(SEG_RV), seg_spec(SEG_RG),
                  tab_spec, tab_spec, tab_spec, tab_spec,
                  pl.BlockSpec((1, RET_HEAD_DIM), lambda b, h: (0, h)),
                  pl.BlockSpec((1, RET_HEAD_DIM), lambda b, h: (0, h))],
        out_specs=pl.BlockSpec((seq, RET_HEAD_DIM), lambda b, h: (b, h)),
        out_shape=jax.ShapeDtypeStruct((t, SEG), BF16),
        scratch_shapes=[pltpu.VMEM((RET_HEAD_DIM, RET_HEAD_DIM), F32)],
        compiler_params=_cparams(2),
        name="retention",
    )(proj, proj, proj, proj, dint, qd, kd, cd, gng, gnb)


SB_Q = 256
SB_K = 256
SB_GROUPS = 2


def _softplus(z):
    return jnp.maximum(z, 0.0) + jnp.log(1.0 + jnp.exp(-jnp.abs(z)))


def _sb_kernel(q_ref, k_ref, v_ref, o_ref):
    qi = pl.program_id(2)
    low = lax.broadcasted_iota(jnp.int32, (SB_Q, LANES), 1) < SB_HEAD_DIM
    tr = lax.broadcasted_iota(jnp.int32, (2 * SB_K, SB_K), 0) & (SB_K - 1)
    tc = lax.broadcasted_iota(jnp.int32, (2 * SB_K, SB_K), 1)
    tri2 = jnp.where(tr >= tc, 1.0, 0.0).astype(BF16)

    def stacked_q(g):
        q = q_ref[:, g * LANES:(g + 1) * LANES]
        zero = jnp.zeros_like(q)
        return jnp.concatenate([jnp.where(low, q, zero), jnp.where(low, zero, q)], axis=0)

    qs = [stacked_q(g) for g in range(SB_GROUPS)]

    def block(kj, carry, diagonal):
        rows = pl.ds(pl.multiple_of(kj * SB_K, SB_K), SB_K)
        if diagonal:
            t_in = lax.broadcasted_iota(jnp.int32, (2 * SB_Q, SB_K), 0) & (SB_Q - 1)
            s_in = lax.broadcasted_iota(jnp.int32, (2 * SB_Q, SB_K), 1)
            mask = s_in < t_in
        groups = range(SB_GROUPS)
        lanes = [slice(g * LANES, (g + 1) * LANES) for g in groups]
        z = [lax.dot_general(qs[g], k_ref[rows, lanes[g]], (((1,), (1,)), ((), ())),
                             preferred_element_type=F32) for g in groups]
        sp = [_softplus(z[g]) for g in groups]
        if diagonal:
            sp = [jnp.where(mask, sp[g], 0.0) for g in groups]
        hi = [sp[g].astype(BF16) for g in groups]
        lo = [(sp[g] - hi[g].astype(F32)).astype(BF16) for g in groups]
        incl = [jnp.dot(jnp.concatenate([hi[g], lo[g]], axis=1), tri2,
                        preferred_element_type=F32) for g in groups]
        a = [jnp.exp(z[g] - incl[g] - carry[g][1]) for g in groups]
        if diagonal:
            a = [jnp.where(mask, a[g], 0.0) for g in groups]
        acc = [carry[g][0] + jnp.dot(a[g].astype(BF16), v_ref[rows, lanes[g]],
                                     preferred_element_type=F32) for g in groups]
        return tuple((acc[g], carry[g][1] + incl[g][:, 0:1]) for g in groups)

    init = tuple((jnp.zeros((2 * SB_Q, LANES), F32), jnp.zeros((2 * SB_Q, 1), F32))
                 for _ in range(SB_GROUPS))
    carry = block(qi, init, True)
    carry = lax.fori_loop(0, qi, lambda i, c: block(qi - 1 - i, c, False), carry)
    for g in range(SB_GROUPS):
        acc = carry[g][0]
        o_ref[:, g * LANES:(g + 1) * LANES] = jnp.where(low, acc[:SB_Q], acc[SB_Q:]).astype(BF16)


def _stick_breaking(proj, batch, seq):
    assert SB_Q == SB_K and seq % SB_Q == 0
    t = proj.shape[0]
    nq = seq // SB_Q
    width = SB_GROUPS * LANES
    steps = SEG // width
    return pl.pallas_call(
        _sb_kernel,
        grid=(batch, steps, nq),
        in_specs=[
            pl.BlockSpec((SB_Q, width), lambda b, p, i: (b * nq + i, SEG_SQ * steps + p)),
            pl.BlockSpec((seq, width), lambda b, p, i: (b, SEG_SK * steps + p)),
            pl.BlockSpec((seq, width), lambda b, p, i: (b, SEG_SV * steps + p)),
        ],
        out_specs=pl.BlockSpec((SB_Q, width), lambda b, p, i: (b * nq + i, p)),
        out_shape=jax.ShapeDtypeStruct((t, SEG), BF16),
        compiler_params=_cparams(3),
        name="stick_breaking",
    )(proj, proj, proj)


def _sgu_kernel(u_ref, v_ref, w_ref, b_ref, o_ref):
    row = lax.broadcasted_iota(jnp.int32, (CHUNK, CHUNK), 0)
    col = lax.broadcasted_iota(jnp.int32, (CHUNK, CHUNK), 1)
    causal = row >= col
    n_chunks = u_ref.shape[0] // CHUNK
    for g in range(SGU_GROUPS):
        wm = jnp.where(causal, w_ref[g], 0.0).astype(BF16)
        bias = b_ref[g]
        gc = slice(g * LANES, (g + 1) * LANES)
        for c in range(n_chunks):
            rc = slice(c * CHUNK, (c + 1) * CHUNK)
            sv = jnp.dot(wm, v_ref[rc, gc], preferred_element_type=F32) + bias
            o_ref[rc, gc] = (u_ref[rc, gc].astype(F32) * sv).astype(BF16)


def _sgu(proj, w_s, b_tab, tm):
    t = proj.shape[0]
    return pl.pallas_call(
        _sgu_kernel,
        grid=(t // tm,),
        in_specs=[
            pl.BlockSpec((tm, SEG), lambda i: (i, SEG_GU)),
            pl.BlockSpec((tm, SEG), lambda i: (i, SEG_GV)),
            _resident((SGU_GROUPS, CHUNK, CHUNK)),
            _resident((SGU_GROUPS, CHUNK, LANES)),
        ],
        out_specs=pl.BlockSpec((tm, SEG), lambda i: (i, 0)),
        out_shape=jax.ShapeDtypeStruct((t, SEG), BF16),
        compiler_params=_cparams(1),
        name="sgu",
    )(proj, proj, w_s, b_tab)


def _merge_kernel(x_ref, ret_ref, sb_ref, sg_ref, g0, g1, g2, g3, g4, g5,
                  pr_ref, ps_ref, pg_ref, wo_ref, lng_ref, lnb_ref, o_ref):
    gates = ((g0, g1), (g2, g3), (g4, g5))
    branches = ((ret_ref, pr_ref), (sb_ref, ps_ref), (sg_ref, pg_ref))
    halves = []
    for half in range(D_MODEL // SEG):
        cols = slice(half * SEG, (half + 1) * SEG)
        m = None
        for (a_ref, p_ref), gate in zip(branches, gates):
            term = gate[half][...].astype(F32) * jnp.dot(
                a_ref[...], p_ref[:, cols], preferred_element_type=F32)
            m = term if m is None else m + term
        halves.append(m.astype(BF16))
    y = (jnp.dot(halves[0], wo_ref[:SEG, :], preferred_element_type=F32)
         + jnp.dot(halves[1], wo_ref[SEG:, :], preferred_element_type=F32))
    o_ref[...] = _layer_norm(DEEPNORM_ALPHA * x_ref[...] + y, lng_ref[...], lnb_ref[...])


def _merge(x2, ret, sb, sg, proj, p_ret, p_sb, p_sgu, w_out, lng, lnb, tm):
    t = x2.shape[0]
    row_spec = lambda w: pl.BlockSpec((tm, w), lambda i: (i, 0))
    gate_specs = [pl.BlockSpec((tm, SEG), functools.partial(lambda i, j: (i, j), j=SEG_GATE0 + j))
                  for j in range(6)]
    return pl.pallas_call(
        _merge_kernel,
        grid=(t // tm,),
        in_specs=[row_spec(D_MODEL), row_spec(SEG), row_spec(SEG), row_spec(SEG)]
        + gate_specs
        + [_resident((SEG, D_MODEL))] * 3
        + [_resident((D_MODEL, D_MODEL)), _resident((1, D_MODEL)), _resident((1, D_MODEL))],
        out_specs=row_spec(D_MODEL),
        out_shape=jax.ShapeDtypeStruct((t, D_MODEL), F32),
        compiler_params=_cparams(1),
        name="merge_out",
    )(x2, ret, sb, sg, *([proj] * 6), p_ret, p_sb, p_sgu, w_out, lng, lnb)


FF_CHUNK = 1024


def _mlp_kernel(x_ref, wu_ref, wd_ref, lng_ref, lnb_ref, o_ref):
    x = x_ref[...]
    xb = x.astype(BF16)
    acc = None
    for c in range(D_FF // FF_CHUNK):
        cols = slice(c * FF_CHUNK, (c + 1) * FF_CHUNK)
        h = jnp.maximum(jnp.dot(xb, wu_ref[:, cols], preferred_element_type=F32), 0.0)
        part = jnp.dot((h * h).astype(BF16), wd_ref[cols, :], preferred_element_type=F32)
        acc = part if acc is None else acc + part
    o_ref[...] = _layer_norm(DEEPNORM_ALPHA * x + acc, lng_ref[...], lnb_ref[...])


def _mlp(x2, w_up, w_down, lng, lnb, tm):
    t = x2.shape[0]
    return pl.pallas_call(
        _mlp_kernel,
        grid=(t // tm,),
        in_specs=[pl.BlockSpec((tm, D_MODEL), lambda i: (i, 0)),
                  _resident((D_MODEL, D_FF)), _resident((D_FF, D_MODEL)),
                  _resident((1, D_MODEL)), _resident((1, D_MODEL))],
        out_specs=pl.BlockSpec((tm, D_MODEL), lambda i: (i, 0)),
        out_shape=jax.ShapeDtypeStruct((t, D_MODEL), F32),
        compiler_params=_cparams(1),
        name="mlp",
    )(x2, w_up, w_down, lng, lnb)


def _rotary_tables(seq):
    half = RET_HEAD_DIM // 2
    inv_freq = ROPE_BASE ** (-jnp.arange(half, dtype=F32) / half)
    ang = jnp.arange(seq, dtype=jnp.int32).astype(F32)[:, None] * inv_freq[None, :]
    cos, sin = jnp.cos(ang), jnp.sin(ang)
    return jnp.concatenate([cos, cos], axis=1), jnp.concatenate([-sin, sin], axis=1)


def _retention_tables():
    log_g = jnp.log(1.0 - 2.0 ** (-5.0 - jnp.arange(RET_HEADS, dtype=F32)))
    idx = jnp.arange(CHUNK, dtype=F32)
    diff = idx[:, None] - idx[None, :]
    dint = jnp.where(diff[None] >= 0, jnp.exp(log_g[:, None, None] * diff[None]), 0.0)
    k_decay = jnp.exp(log_g[:, None] * (CHUNK - 1 - idx)[None, :])
    q_decay = jnp.exp(log_g[:, None] * (idx + 1.0)[None, :])
    chunk_decay = jnp.exp(log_g * CHUNK)
    full = (RET_HEADS, CHUNK, RET_HEAD_DIM)
    return (dint.astype(F32),
            jnp.broadcast_to(q_decay[:, :, None], full),
            jnp.broadcast_to(k_decay[:, :, None], full),
            jnp.broadcast_to(chunk_decay[:, None, None], full))


def kernel(x, w_in, ret_gn_g, ret_gn_b, sgu_ln_g, sgu_ln_b, sgu_w, sgu_b, p_ret, p_sb, p_sgu,
           w_out, ln1_g, ln1_b, w_up, w_down, ln2_g, ln2_b):
    batch, seq, d = x.shape
    assert d == D_MODEL and seq % CHUNK == 0
    t = batch * seq
    tm = 512
    assert seq % tm == 0
    cs, sn = _rotary_tables(seq)
    ret_tabs = _retention_tables()
    row = lambda a: a.reshape(1, -1)
    x2 = x.reshape(t, d)
    for l in range(DEPTH):
        proj = _in_proj(x2, w_in[l].astype(BF16), cs, sn, row(sgu_ln_g[l]), row(sgu_ln_b[l]),
                        tm, seq)
        ret = _retention(proj, ret_tabs, row(ret_gn_g[l]), row(ret_gn_b[l]), batch, seq)
        sb = _stick_breaking(proj, batch, seq)
        b_tab = jnp.broadcast_to(sgu_b[l][:, :, None], (SGU_GROUPS, CHUNK, LANES))
        sg = _sgu(proj, sgu_w[l], b_tab, tm)
        x2 = _merge(x2, ret, sb, sg, proj, p_ret[l].astype(BF16), p_sb[l].astype(BF16),
                    p_sgu[l].astype(BF16), w_out[l].astype(BF16), row(ln1_g[l]), row(ln1_b[l]), tm)
        x2 = _mlp(x2, w_up[l].astype(BF16), w_down[l].astype(BF16), row(ln2_g[l]), row(ln2_b[l]), tm)
    return x2.reshape(batch, seq, d)
```

```python
import functools

import jax
import jax.numpy as jnp
from jax import lax
from jax.experimental import pallas as pl
from jax.experimental.pallas import tpu as pltpu

F32 = jnp.float32
BF16 = jnp.bfloat16

D_MODEL = 1024
DEPTH = 2
CHUNK = 128
RET_HEADS = 4
RET_HEAD_DIM = 128
SB_HEADS = 8
SB_HEAD_DIM = 64
SGU_GROUPS = 4
SEG = 512
D_FF = 4 * D_MODEL
N_IN = 9 * SEG + 3 * D_MODEL
ROPE_BASE = 10000.0
LN_EPS = 1e-5
DEEPNORM_ALPHA = (2 * DEPTH) ** 0.25

SEG_RQ, SEG_RK, SEG_RV, SEG_RG, SEG_SQ, SEG_SK, SEG_SV, SEG_GU, SEG_GV = range(9)
SEG_GATE0 = 9
N_SEG = N_IN // SEG

LANES = 128
VMEM_LIMIT = 56 * 1024 * 1024


def _cparams(n_axes):
    return pltpu.CompilerParams(
        dimension_semantics=("arbitrary",) * n_axes,
        vmem_limit_bytes=VMEM_LIMIT)


def _resident(shape):
    zeros = (0,) * len(shape)
    return pl.BlockSpec(shape, lambda *_: zeros, pipeline_mode=pl.Buffered(1))


def _layer_norm(x, g, b):
    mu = jnp.mean(x, axis=-1, keepdims=True)
    xc = x - mu
    var = jnp.mean(xc * xc, axis=-1, keepdims=True)
    return xc * lax.rsqrt(var + LN_EPS) * g + b


def _inproj_kernel(x_ref, w_ref, cs_ref, sn_ref, lng_ref, lnb_ref, o_ref):
    xb = x_ref[...].astype(BF16)
    cs = cs_ref[...]
    sn = sn_ref[...]
    for j in range(N_SEG):
        cols = slice(j * SEG, (j + 1) * SEG)
        acc = jnp.dot(xb, w_ref[:, cols], preferred_element_type=F32)
        if j in (SEG_RQ, SEG_RK):
            scale = RET_HEAD_DIM ** -0.5 if j == SEG_RK else None
            for h in range(RET_HEADS):
                hc = slice(h * RET_HEAD_DIM, (h + 1) * RET_HEAD_DIM)
                xh = acc[:, hc]
                r = xh * cs + pltpu.roll(xh, RET_HEAD_DIM // 2, axis=1) * sn
                if scale is not None:
                    r = r * scale
                o_ref[:, j * SEG + h * RET_HEAD_DIM:j * SEG + (h + 1) * RET_HEAD_DIM] = r.astype(BF16)
            continue
        if j == SEG_RG:
            res = jax.nn.silu(acc)
        elif j == SEG_SQ:
            res = acc * (SB_HEAD_DIM ** -0.5)
        elif j == SEG_GU:
            res = jax.nn.gelu(acc)
        elif j == SEG_GV:
            res = _layer_norm(jax.nn.gelu(acc), lng_ref[...], lnb_ref[...])
        elif j >= SEG_GATE0:
            res = jax.nn.sigmoid(acc)
        else:
            res = acc
        o_ref[:, cols] = res.astype(BF16)


def _in_proj(x2, w, cs, sn, lng, lnb, tm, seq):
    t = x2.shape[0]
    pos_blocks = seq // tm
    return pl.pallas_call(
        _inproj_kernel,
        grid=(t // tm,),
        in_specs=[
            pl.BlockSpec((tm, D_MODEL), lambda i: (i, 0)),
            _resident((D_MODEL, N_IN)),
            pl.BlockSpec((tm, LANES), lambda i: (i % pos_blocks, 0)),
            pl.BlockSpec((tm, LANES), lambda i: (i % pos_blocks, 0)),
            _resident((1, SEG)),
            _resident((1, SEG)),
        ],
        out_specs=pl.BlockSpec((tm, N_IN), lambda i: (i, 0)),
        out_shape=jax.ShapeDtypeStruct((t, N_IN), BF16),
        compiler_params=_cparams(1),
        name="in_proj",
    )(x2, w, cs, sn, lng, lnb)


RET_ROWS = 1024


def _ret_kernel(q_ref, k_ref, v_ref, g_ref, dint_ref, qd_ref, kd_ref, cd_ref,
                gng_ref, gnb_ref, o_ref, state_ref):
    @pl.when(pl.program_id(1) == 0)
    def _():
        state_ref[...] = jnp.zeros_like(state_ref)

    heads = range(RET_HEADS)
    cols = [slice(h * RET_HEAD_DIM, (h + 1) * RET_HEAD_DIM) for h in heads]
    nt = (((1,), (1,)), ((), ()))
    tn = (((0,), (0,)), ((), ()))

    def body(n, carry):
        rows = pl.ds(pl.multiple_of(n * CHUNK, CHUNK), CHUNK)
        q = [q_ref[rows, cols[h]] for h in heads]
        k = [k_ref[rows, cols[h]] for h in heads]
        v = [v_ref[rows, cols[h]] for h in heads]
        s = [lax.dot_general(q[h], k[h], nt, preferred_element_type=F32) * dint_ref[h]
             for h in heads]
        kdec = [(k[h].astype(F32) * kd_ref[h]).astype(BF16) for h in heads]
        kv = [lax.dot_general(kdec[h], v[h], tn, preferred_element_type=F32) for h in heads]
        qdec = [(q[h].astype(F32) * qd_ref[h]).astype(BF16) for h in heads]
        st = [state_ref[h] for h in heads]
        inter = [jnp.dot(qdec[h], st[h].astype(BF16), preferred_element_type=F32) for h in heads]
        intra = [jnp.dot(s[h].astype(BF16), v[h], preferred_element_type=F32) for h in heads]
        for h in heads:
            state_ref[h] = st[h] * cd_ref[h] + kv[h]
            y = _layer_norm(intra[h] + inter[h], gng_ref[:, cols[h]], gnb_ref[:, cols[h]])
            o_ref[rows, cols[h]] = (g_ref[rows, cols[h]].astype(F32) * y).astype(BF16)
        return carry

    lax.fori_loop(0, q_ref.shape[0] // CHUNK, body, 0)


def _retention(proj, tabs, gng, gnb, batch, seq):
    dint, qd, kd, cd = tabs
    t = proj.shape[0]
    steps = seq // RET_ROWS

    def seg_spec(seg):
        return pl.BlockSpec((RET_ROWS, SEG), lambda b, i: (b * steps + i, seg))

    tab_spec = _resident((RET_HEADS, CHUNK, RET_HEAD_DIM))
    return pl.pallas_call(
        _ret_kernel,
        grid=(batch, steps),
        in_specs=[seg_spec(SEG_RQ), seg_spec(SEG_RK), seg_spec(SEG_RV), seg_spec(SEG_RG),
                  tab_spec, tab_spec, tab_spec, tab_spec,
                  _resident((1, SEG)), _resident((1, SEG))],
        out_specs=pl.BlockSpec((RET_ROWS, SEG), lambda b, i: (b * steps + i, 0)),
        out_shape=jax.ShapeDtypeStruct((t, SEG), BF16),
        scratch_shapes=[pltpu.VMEM((RET_HEADS, RET_HEAD_DIM, RET_HEAD_DIM), F32)],
        compiler_params=_cparams(2),
        name="retention",
    )(proj, proj, proj, proj, dint, qd, kd, cd, gng, gnb)


SB_Q = 256
SB_K = 256
SB_GROUPS = 2
SB_UNDERFLOW = 104.0


def _softplus(z):
    return jnp.maximum(z, 0.0) + jnp.log(1.0 + jnp.exp(-jnp.abs(z)))


def _sb_kernel(q_ref, k_ref, v_ref, o_ref):
    qi = pl.program_id(2)
    low = lax.broadcasted_iota(jnp.int32, (SB_Q, LANES), 1) < SB_HEAD_DIM
    tr = lax.broadcasted_iota(jnp.int32, (2 * SB_K, SB_K), 0) & (SB_K - 1)
    tc = lax.broadcasted_iota(jnp.int32, (2 * SB_K, SB_K), 1)
    tri2 = jnp.where(tr >= tc, 1.0, 0.0).astype(BF16)

    def stacked_q(g):
        q = q_ref[:, g * LANES:(g + 1) * LANES]
        zero = jnp.zeros_like(q)
        return jnp.concatenate([jnp.where(low, q, zero), jnp.where(low, zero, q)], axis=0)

    qs = [stacked_q(g) for g in range(SB_GROUPS)]

    def block(kj, carry, diagonal):
        rows = pl.ds(pl.multiple_of(kj * SB_K, SB_K), SB_K)
        if diagonal:
            t_in = lax.broadcasted_iota(jnp.int32, (2 * SB_Q, SB_K), 0) & (SB_Q - 1)
            s_in = lax.broadcasted_iota(jnp.int32, (2 * SB_Q, SB_K), 1)
            mask = s_in < t_in
        groups = range(SB_GROUPS)
        lanes = [slice(g * LANES, (g + 1) * LANES) for g in groups]
        z = [lax.dot_general(qs[g], k_ref[rows, lanes[g]], (((1,), (1,)), ((), ())),
                             preferred_element_type=F32) for g in groups]
        sp = [_softplus(z[g]) for g in groups]
        if diagonal:
            sp = [jnp.where(mask, sp[g], 0.0) for g in groups]
        hi = [sp[g].astype(BF16) for g in groups]
        lo = [(sp[g] - hi[g].astype(F32)).astype(BF16) for g in groups]
        incl = [jnp.dot(jnp.concatenate([hi[g], lo[g]], axis=1), tri2,
                        preferred_element_type=F32) for g in groups]
        a = [jnp.exp(z[g] - incl[g] - carry[g][1]) for g in groups]
        if diagonal:
            a = [jnp.where(mask, a[g], 0.0) for g in groups]
        acc = [carry[g][0] + jnp.dot(a[g].astype(BF16), v_ref[rows, lanes[g]],
                                     preferred_element_type=F32) for g in groups]
        return tuple((acc[g], carry[g][1] + incl[g][:, 0:1]) for g in groups)

    init = tuple((jnp.zeros((2 * SB_Q, LANES), F32), jnp.zeros((2 * SB_Q, 1), F32))
                 for _ in range(SB_GROUPS))
    carry = block(qi, init, True)

    def live(c):
        return jnp.logical_and(c[0] < qi, c[1] <= SB_UNDERFLOW)

    def step(c):
        nxt = block(qi - 1 - c[0], c[2], False)
        low_run = functools.reduce(jnp.minimum, [jnp.min(nxt[g][1]) for g in range(SB_GROUPS)])
        return c[0] + 1, low_run, nxt

    carry = lax.while_loop(live, step, (jnp.int32(0), jnp.float32(0.0), carry))[2]
    for g in range(SB_GROUPS):
        acc = carry[g][0]
        o_ref[:, g * LANES:(g + 1) * LANES] = jnp.where(low, acc[:SB_Q], acc[SB_Q:]).astype(BF16)


def _stick_breaking(proj, batch, seq):
    assert SB_Q == SB_K and seq % SB_Q == 0
    t = proj.shape[0]
    nq = seq // SB_Q
    width = SB_GROUPS * LANES
    steps = SEG // width
    return pl.pallas_call(
        _sb_kernel,
        grid=(batch, steps, nq),
        in_specs=[
            pl.BlockSpec((SB_Q, width), lambda b, p, i: (b * nq + i, SEG_SQ * steps + p)),
            pl.BlockSpec((seq, width), lambda b, p, i: (b, SEG_SK * steps + p)),
            pl.BlockSpec((seq, width), lambda b, p, i: (b, SEG_SV * steps + p)),
        ],
        out_specs=pl.BlockSpec((SB_Q, width), lambda b, p, i: (b * nq + i, p)),
        out_shape=jax.ShapeDtypeStruct((t, SEG), BF16),
        compiler_params=_cparams(3),
        name="stick_breaking",
    )(proj, proj, proj)


def _sgu_kernel(u_ref, v_ref, w_ref, b_ref, o_ref):
    row = lax.broadcasted_iota(jnp.int32, (CHUNK, CHUNK), 0)
    col = lax.broadcasted_iota(jnp.int32, (CHUNK, CHUNK), 1)
    causal = row >= col
    n_chunks = u_ref.shape[0] // CHUNK
    for g in range(SGU_GROUPS):
        wm = jnp.where(causal, w_ref[g], 0.0).astype(BF16)
        bias = b_ref[g]
        gc = slice(g * LANES, (g + 1) * LANES)
        for c in range(n_chunks):
            rc = slice(c * CHUNK, (c + 1) * CHUNK)
            sv = jnp.dot(wm, v_ref[rc, gc], preferred_element_type=F32) + bias
            o_ref[rc, gc] = (u_ref[rc, gc].astype(F32) * sv).astype(BF16)


def _sgu(proj, w_s, b_tab, tm):
    t = proj.shape[0]
    return pl.pallas_call(
        _sgu_kernel,
        grid=(t // tm,),
        in_specs=[
            pl.BlockSpec((tm, SEG), lambda i: (i, SEG_GU)),
            pl.BlockSpec((tm, SEG), lambda i: (i, SEG_GV)),
            _resident((SGU_GROUPS, CHUNK, CHUNK)),
            _resident((SGU_GROUPS, CHUNK, LANES)),
        ],
        out_specs=pl.BlockSpec((tm, SEG), lambda i: (i, 0)),
        out_shape=jax.ShapeDtypeStruct((t, SEG), BF16),
        compiler_params=_cparams(1),
        name="sgu",
    )(proj, proj, w_s, b_tab)


def _merge_kernel(x_ref, ret_ref, sb_ref, sg_ref, g0, g1, g2, g3, g4, g5,
                  pr_ref, ps_ref, pg_ref, wo_ref, lng_ref, lnb_ref, o_ref):
    gates = ((g0, g1), (g2, g3), (g4, g5))
    branches = ((ret_ref, pr_ref), (sb_ref, ps_ref), (sg_ref, pg_ref))
    halves = []
    for half in range(D_MODEL // SEG):
        cols = slice(half * SEG, (half + 1) * SEG)
        m = None
        for (a_ref, p_ref), gate in zip(branches, gates):
            term = gate[half][...].astype(F32) * jnp.dot(
                a_ref[...], p_ref[:, cols], preferred_element_type=F32)
            m = term if m is None else m + term
        halves.append(m.astype(BF16))
    y = (jnp.dot(halves[0], wo_ref[:SEG, :], preferred_element_type=F32)
         + jnp.dot(halves[1], wo_ref[SEG:, :], preferred_element_type=F32))
    o_ref[...] = _layer_norm(DEEPNORM_ALPHA * x_ref[...] + y, lng_ref[...], lnb_ref[...])


def _merge(x2, ret, sb, sg, proj, p_ret, p_sb, p_sgu, w_out, lng, lnb, tm):
    t = x2.shape[0]
    row_spec = lambda w: pl.BlockSpec((tm, w), lambda i: (i, 0))
    gate_specs = [pl.BlockSpec((tm, SEG), functools.partial(lambda i, j: (i, j), j=SEG_GATE0 + j))
                  for j in range(6)]
    return pl.pallas_call(
        _merge_kernel,
        grid=(t // tm,),
        in_specs=[row_spec(D_MODEL), row_spec(SEG), row_spec(SEG), row_spec(SEG)]
        + gate_specs
        + [_resident((SEG, D_MODEL))] * 3
        + [_resident((D_MODEL, D_MODEL)), _resident((1, D_MODEL)), _resident((1, D_MODEL))],
        out_specs=row_spec(D_MODEL),
        out_shape=jax.ShapeDtypeStruct((t, D_MODEL), F32),
        compiler_params=_cparams(1),
        name="merge_out",
    )(x2, ret, sb, sg, *([proj] * 6), p_ret, p_sb, p_sgu, w_out, lng, lnb)


FF_CHUNK = 1024


def _mlp_kernel(x_ref, wu_ref, wd_ref, lng_ref, lnb_ref, o_ref):
    x = x_ref[...]
    xb = x.astype(BF16)
    acc = None
    for c in range(D_FF // FF_CHUNK):
        cols = slice(c * FF_CHUNK, (c + 1) * FF_CHUNK)
        h = jnp.maximum(jnp.dot(xb, wu_ref[:, cols], preferred_element_type=F32), 0.0)
        part = jnp.dot((h * h).astype(BF16), wd_ref[cols, :], preferred_element_type=F32)
        acc = part if acc is None else acc + part
    o_ref[...] = _layer_norm(DEEPNORM_ALPHA * x + acc, lng_ref[...], lnb_ref[...])


def _mlp(x2, w_up, w_down, lng, lnb, tm):
    t = x2.shape[0]
    return pl.pallas_call(
        _mlp_kernel,
        grid=(t // tm,),
        in_specs=[pl.BlockSpec((tm, D_MODEL), lambda i: (i, 0)),
                  _resident((D_MODEL, D_FF)), _resident((D_FF, D_MODEL)),
                  _resident((1, D_MODEL)), _resident((1, D_MODEL))],
        out_specs=pl.BlockSpec((tm, D_MODEL), lambda i: (i, 0)),
        out_shape=jax.ShapeDtypeStruct((t, D_MODEL), F32),
        compiler_params=_cparams(1),
        name="mlp",
    )(x2, w_up, w_down, lng, lnb)


def _rotary_tables(seq):
    half = RET_HEAD_DIM // 2
    inv_freq = ROPE_BASE ** (-jnp.arange(half, dtype=F32) / half)
    ang = jnp.arange(seq, dtype=jnp.int32).astype(F32)[:, None] * inv_freq[None, :]
    cos, sin = jnp.cos(ang), jnp.sin(ang)
    return jnp.concatenate([cos, cos], axis=1), jnp.concatenate([-sin, sin], axis=1)


def _retention_tables():
    log_g = jnp.log(1.0 - 2.0 ** (-5.0 - jnp.arange(RET_HEADS, dtype=F32)))
    idx = jnp.arange(CHUNK, dtype=F32)
    diff = idx[:, None] - idx[None, :]
    dint = jnp.where(diff[None] >= 0, jnp.exp(log_g[:, None, None] * diff[None]), 0.0)
    k_decay = jnp.exp(log_g[:, None] * (CHUNK - 1 - idx)[None, :])
    q_decay = jnp.exp(log_g[:, None] * (idx + 1.0)[None, :])
    chunk_decay = jnp.exp(log_g * CHUNK)
    full = (RET_HEADS, CHUNK, RET_HEAD_DIM)
    return (dint.astype(F32),
            jnp.broadcast_to(q_decay[:, :, None], full),
            jnp.broadcast_to(k_decay[:, :, None], full),
            jnp.broadcast_to(chunk_decay[:, None, None], full))


def kernel(x, w_in, ret_gn_g, ret_gn_b, sgu_ln_g, sgu_ln_b, sgu_w, sgu_b, p_ret, p_sb, p_sgu,
           w_out, ln1_g, ln1_b, w_up, w_down, ln2_g, ln2_b):
    batch, seq, d = x.shape
    assert d == D_MODEL and seq % CHUNK == 0
    t = batch * seq
    tm = 512
    assert seq % tm == 0
    cs, sn = _rotary_tables(seq)
    ret_tabs = _retention_tables()
    row = lambda a: a.reshape(1, -1)
    x2 = x.reshape(t, d)
    for l in range(DEPTH):
        proj = _in_proj(x2, w_in[l].astype(BF16), cs, sn, row(sgu_ln_g[l]), row(sgu_ln_b[l]),
                        tm, seq)
        ret = _retention(proj, ret_tabs, row(ret_gn_g[l]), row(ret_gn_b[l]), batch, seq)
        sb = _stick_breaking(proj, batch, seq)
        b_tab = jnp.broadcast_to(sgu_b[l][:, :, None], (SGU_GROUPS, CHUNK, LANES))
        sg = _sgu(proj, sgu_w[l], b_tab, tm)
        x2 = _merge(x2, ret, sb, sg, proj, p_ret[l].astype(BF16), p_sb[l].astype(BF16),
                    p_sgu[l].astype(BF16), w_out[l].astype(BF16), row(ln1_g[l]), row(ln1_b[l]), tm)
        x2 = _mlp(x2, w_up[l].astype(BF16), w_down[l].astype(BF16), row(ln2_g[l]), row(ln2_b[l]), tm)
    return x2.reshape(batch, seq, d)
```

```python
import functools

import jax
import jax.numpy as jnp
from jax import lax
from jax.experimental import pallas as pl
from jax.experimental.pallas import tpu as pltpu

F32 = jnp.float32
BF16 = jnp.bfloat16

D_MODEL = 1024
DEPTH = 2
CHUNK = 128
RET_HEADS = 4
RET_HEAD_DIM = 128
SB_HEADS = 8
SB_HEAD_DIM = 64
SGU_GROUPS = 4
SEG = 512
D_FF = 4 * D_MODEL
N_IN = 9 * SEG + 3 * D_MODEL
ROPE_BASE = 10000.0
LN_EPS = 1e-5
DEEPNORM_ALPHA = (2 * DEPTH) ** 0.25

SEG_RQ, SEG_RK, SEG_RV, SEG_RG, SEG_SQ, SEG_SK, SEG_SV, SEG_GU, SEG_GV = range(9)
SEG_GATE0 = 9
N_SEG = N_IN // SEG

LANES = 128
VMEM_LIMIT = 56 * 1024 * 1024


def _cparams(n_axes):
    return pltpu.CompilerParams(
        dimension_semantics=("arbitrary",) * n_axes,
        vmem_limit_bytes=VMEM_LIMIT)


def _resident(shape):
    zeros = (0,) * len(shape)
    return pl.BlockSpec(shape, lambda *_: zeros, pipeline_mode=pl.Buffered(1))


def _of_layer(shape, l):
    zeros = (0,) * len(shape)
    return pl.BlockSpec((None,) + tuple(shape), lambda *_: (l,) + zeros,
                        pipeline_mode=pl.Buffered(1))


def _layer_norm(x, g, b):
    mu = jnp.mean(x, axis=-1, keepdims=True)
    xc = x - mu
    var = jnp.mean(xc * xc, axis=-1, keepdims=True)
    return xc * lax.rsqrt(var + LN_EPS) * g + b


def _inproj_kernel(x_ref, w_ref, cs_ref, sn_ref, lng_ref, lnb_ref, o_ref):
    xb = x_ref[...].astype(BF16)
    cs = cs_ref[...]
    sn = sn_ref[...]
    for j in range(N_SEG):
        cols = slice(j * SEG, (j + 1) * SEG)
        acc = jnp.dot(xb, w_ref[:, cols], preferred_element_type=F32)
        if j in (SEG_RQ, SEG_RK):
            scale = RET_HEAD_DIM ** -0.5 if j == SEG_RK else None
            for h in range(RET_HEADS):
                hc = slice(h * RET_HEAD_DIM, (h + 1) * RET_HEAD_DIM)
                xh = acc[:, hc]
                r = xh * cs + pltpu.roll(xh, RET_HEAD_DIM // 2, axis=1) * sn
                if scale is not None:
                    r = r * scale
                o_ref[:, j * SEG + h * RET_HEAD_DIM:j * SEG + (h + 1) * RET_HEAD_DIM] = r.astype(BF16)
            continue
        if j == SEG_RG:
            res = jax.nn.silu(acc)
        elif j == SEG_SQ:
            res = acc * (SB_HEAD_DIM ** -0.5 * LOG2E)
        elif j == SEG_GU:
            res = jax.nn.gelu(acc)
        elif j == SEG_GV:
            res = _layer_norm(jax.nn.gelu(acc), lng_ref[...], lnb_ref[...])
        elif j >= SEG_GATE0:
            res = jax.nn.sigmoid(acc)
        else:
            res = acc
        o_ref[:, cols] = res.astype(BF16)


def _in_proj(x2, w, cs, sn, lng, lnb, l, tm, seq):
    t = x2.shape[0]
    pos_blocks = seq // tm
    return pl.pallas_call(
        _inproj_kernel,
        grid=(t // tm,),
        in_specs=[
            pl.BlockSpec((tm, D_MODEL), lambda i: (i, 0)),
            _of_layer((D_MODEL, N_IN), l),
            pl.BlockSpec((tm, LANES), lambda i: (i % pos_blocks, 0)),
            pl.BlockSpec((tm, LANES), lambda i: (i % pos_blocks, 0)),
            _of_layer((1, SEG), l),
            _of_layer((1, SEG), l),
        ],
        out_specs=pl.BlockSpec((tm, N_IN), lambda i: (i, 0)),
        out_shape=jax.ShapeDtypeStruct((t, N_IN), BF16),
        compiler_params=_cparams(1),
        name="in_proj",
    )(x2, w, cs, sn, lng, lnb)


RET_ROWS = 1024


def _ret_kernel(q_ref, k_ref, v_ref, g_ref, dint_ref, qd_ref, kd_ref, cd_ref,
                gng_ref, gnb_ref, o_ref, state_ref):
    @pl.when(pl.program_id(1) == 0)
    def _():
        state_ref[...] = jnp.zeros_like(state_ref)

    heads = range(RET_HEADS)
    cols = [slice(h * RET_HEAD_DIM, (h + 1) * RET_HEAD_DIM) for h in heads]
    nt = (((1,), (1,)), ((), ()))
    tn = (((0,), (0,)), ((), ()))

    def body(n, carry):
        rows = pl.ds(pl.multiple_of(n * CHUNK, CHUNK), CHUNK)
        q = [q_ref[rows, cols[h]] for h in heads]
        k = [k_ref[rows, cols[h]] for h in heads]
        v = [v_ref[rows, cols[h]] for h in heads]
        s = [lax.dot_general(q[h], k[h], nt, preferred_element_type=F32) * dint_ref[h]
             for h in heads]
        kdec = [(k[h].astype(F32) * kd_ref[h]).astype(BF16) for h in heads]
        kv = [lax.dot_general(kdec[h], v[h], tn, preferred_element_type=F32) for h in heads]
        qdec = [(q[h].astype(F32) * qd_ref[h]).astype(BF16) for h in heads]
        st = [state_ref[h] for h in heads]
        inter = [jnp.dot(qdec[h], st[h].astype(BF16), preferred_element_type=F32) for h in heads]
        intra = [jnp.dot(s[h].astype(BF16), v[h], preferred_element_type=F32) for h in heads]
        for h in heads:
            state_ref[h] = st[h] * cd_ref[h] + kv[h]
            y = _layer_norm(intra[h] + inter[h], gng_ref[:, cols[h]], gnb_ref[:, cols[h]])
            o_ref[rows, cols[h]] = (g_ref[rows, cols[h]].astype(F32) * y).astype(BF16)
        return carry

    lax.fori_loop(0, q_ref.shape[0] // CHUNK, body, 0)


def _retention(proj, tabs, gng, gnb, l, batch, seq):
    dint, qd, kd, cd = tabs
    t = proj.shape[0]
    steps = seq // RET_ROWS

    def seg_spec(seg):
        return pl.BlockSpec((RET_ROWS, SEG), lambda b, i: (b * steps + i, seg))

    tab_spec = _resident((RET_HEADS, CHUNK, RET_HEAD_DIM))
    return pl.pallas_call(
        _ret_kernel,
        grid=(batch, steps),
        in_specs=[seg_spec(SEG_RQ), seg_spec(SEG_RK), seg_spec(SEG_RV), seg_spec(SEG_RG),
                  tab_spec, tab_spec, tab_spec, tab_spec,
                  _of_layer((1, SEG), l), _of_layer((1, SEG), l)],
        out_specs=pl.BlockSpec((RET_ROWS, SEG), lambda b, i: (b * steps + i, 0)),
        out_shape=jax.ShapeDtypeStruct((t, SEG), BF16),
        scratch_shapes=[pltpu.VMEM((RET_HEADS, RET_HEAD_DIM, RET_HEAD_DIM), F32)],
        compiler_params=_cparams(2),
        name="retention",
    )(proj, proj, proj, proj, dint, qd, kd, cd, gng, gnb)


SB_Q = 256
SB_K = 256
SB_GROUPS = 2
LOG2E = 1.4426950408889634
SB_UNDERFLOW = 152.0
SB_MASKED = -1e30
SB_CLAMP = 126.0


def _sb_kernel(q_ref, k_ref, v_ref, o_ref):
    qi = pl.program_id(2)
    groups = range(SB_GROUPS)
    lanes = [slice(g * LANES, (g + 1) * LANES) for g in groups]
    low = lax.broadcasted_iota(jnp.int32, (SB_Q, LANES), 1) < SB_HEAD_DIM
    tr = lax.broadcasted_iota(jnp.int32, (SB_K, SB_K), 0)
    tc = lax.broadcasted_iota(jnp.int32, (SB_K, SB_K), 1)
    tri = jnp.where(tr >= tc, 1.0, 0.0).astype(BF16)

    def stacked_q(g):
        q = q_ref[:, lanes[g]]
        zero = jnp.zeros_like(q)
        return jnp.concatenate([jnp.where(low, q, zero), jnp.where(low, zero, q)], axis=0)

    qs = [stacked_q(g) for g in groups]

    def key_rows(kj):
        return pl.ds(pl.multiple_of(kj * SB_K, SB_K), SB_K)

    def scores(rows, g):
        return lax.dot_general(qs[g], k_ref[rows, lanes[g]], (((1,), (1,)), ((), ())),
                               preferred_element_type=F32)

    def suffix_sums(z):
        sp = jnp.maximum(jnp.log(1.0 + jnp.exp2(jnp.minimum(z, SB_CLAMP))) * LOG2E, z)
        return jnp.dot(sp.astype(BF16), tri, preferred_element_type=F32)

    def weighted(z, incl, run, rows, g):
        a = jnp.exp2(z - incl - run)
        return jnp.dot(a.astype(BF16), v_ref[rows, lanes[g]], preferred_element_type=F32)

    rows_d = key_rows(qi)
    rows_p = key_rows(jnp.maximum(qi - 1, 0))
    t_in = lax.broadcasted_iota(jnp.int32, (2 * SB_Q, SB_K), 0) & (SB_Q - 1)
    s_in = lax.broadcasted_iota(jnp.int32, (2 * SB_Q, SB_K), 1)
    causal = s_in < t_in
    z_d = [jnp.where(causal, scores(rows_d, g), SB_MASKED) for g in groups]
    z_p = [scores(rows_p, g) for g in groups]
    incl_d = [suffix_sums(z_d[g]) for g in groups]
    incl_p = [suffix_sums(z_p[g]) for g in groups]
    no_prev = jnp.where(qi == 0, -SB_MASKED, 0.0)
    run_p = [incl_d[g][:, 0:1] + no_prev for g in groups]
    acc = [weighted(z_d[g], incl_d[g], 0.0, rows_d, g)
           + weighted(z_p[g], incl_p[g], run_p[g], rows_p, g) for g in groups]
    run = [run_p[g] + incl_p[g][:, 0:1] for g in groups]

    def lowest(run):
        return functools.reduce(jnp.minimum, [jnp.min(r) for r in run])

    def live(c):
        return jnp.logical_and(c[0] < qi, c[1] <= SB_UNDERFLOW)

    def step(c):
        i, _, acc, run = c
        rows = key_rows(qi - 1 - i)
        z = [scores(rows, g) for g in groups]
        incl = [suffix_sums(z[g]) for g in groups]
        acc = [acc[g] + weighted(z[g], incl[g], run[g], rows, g) for g in groups]
        run = [run[g] + incl[g][:, 0:1] for g in groups]
        return i + 1, lowest(run), acc, run

    _, _, acc, _ = lax.while_loop(live, step, (jnp.int32(1), lowest(run), acc, run))
    for g in groups:
        o_ref[:, lanes[g]] = jnp.where(low, acc[g][:SB_Q], acc[g][SB_Q:]).astype(BF16)


def _stick_breaking(proj, batch, seq):
    assert SB_Q == SB_K and seq % SB_Q == 0
    t = proj.shape[0]
    nq = seq // SB_Q
    width = SB_GROUPS * LANES
    steps = SEG // width
    return pl.pallas_call(
        _sb_kernel,
        grid=(batch, steps, nq),
        in_specs=[
            pl.BlockSpec((SB_Q, width), lambda b, p, i: (b * nq + i, SEG_SQ * steps + p)),
            pl.BlockSpec((seq, width), lambda b, p, i: (b, SEG_SK * steps + p)),
            pl.BlockSpec((seq, width), lambda b, p, i: (b, SEG_SV * steps + p)),
        ],
        out_specs=pl.BlockSpec((SB_Q, width), lambda b, p, i: (b * nq + i, p)),
        out_shape=jax.ShapeDtypeStruct((t, SEG), BF16),
        compiler_params=_cparams(3),
        name="stick_breaking",
    )(proj, proj, proj)


def _sgu_kernel(u_ref, v_ref, w_ref, b_ref, o_ref):
    row = lax.broadcasted_iota(jnp.int32, (CHUNK, CHUNK), 0)
    col = lax.broadcasted_iota(jnp.int32, (CHUNK, CHUNK), 1)
    causal = row >= col
    n_chunks = u_ref.shape[0] // CHUNK
    for g in range(SGU_GROUPS):
        wm = jnp.where(causal, w_ref[g], 0.0).astype(BF16)
        bias = b_ref[g]
        gc = slice(g * LANES, (g + 1) * LANES)
        for c in range(n_chunks):
            rc = slice(c * CHUNK, (c + 1) * CHUNK)
            sv = jnp.dot(wm, v_ref[rc, gc], preferred_element_type=F32) + bias
            o_ref[rc, gc] = (u_ref[rc, gc].astype(F32) * sv).astype(BF16)


def _sgu(proj, w_s, b_tab, l, tm):
    t = proj.shape[0]
    return pl.pallas_call(
        _sgu_kernel,
        grid=(t // tm,),
        in_specs=[
            pl.BlockSpec((tm, SEG), lambda i: (i, SEG_GU)),
            pl.BlockSpec((tm, SEG), lambda i: (i, SEG_GV)),
            _of_layer((SGU_GROUPS, CHUNK, CHUNK), l),
            _of_layer((SGU_GROUPS, CHUNK, LANES), l),
        ],
        out_specs=pl.BlockSpec((tm, SEG), lambda i: (i, 0)),
        out_shape=jax.ShapeDtypeStruct((t, SEG), BF16),
        compiler_params=_cparams(1),
        name="sgu",
    )(proj, proj, w_s, b_tab)


def _merge_kernel(x_ref, ret_ref, sb_ref, sg_ref, g0, g1, g2, g3, g4, g5,
                  pr_ref, ps_ref, pg_ref, wo_ref, lng_ref, lnb_ref, o_ref):
    gates = ((g0, g1), (g2, g3), (g4, g5))
    branches = ((ret_ref, pr_ref), (sb_ref, ps_ref), (sg_ref, pg_ref))
    halves = []
    for half in range(D_MODEL // SEG):
        cols = slice(half * SEG, (half + 1) * SEG)
        m = None
        for (a_ref, p_ref), gate in zip(branches, gates):
            term = gate[half][...].astype(F32) * jnp.dot(
                a_ref[...], p_ref[:, cols], preferred_element_type=F32)
            m = term if m is None else m + term
        halves.append(m.astype(BF16))
    y = (jnp.dot(halves[0], wo_ref[:SEG, :], preferred_element_type=F32)
         + jnp.dot(halves[1], wo_ref[SEG:, :], preferred_element_type=F32))
    o_ref[...] = _layer_norm(DEEPNORM_ALPHA * x_ref[...] + y, lng_ref[...], lnb_ref[...])


def _merge(x2, ret, sb, sg, proj, p_ret, p_sb, p_sgu, w_out, lng, lnb, l, tm):
    t = x2.shape[0]
    row_spec = lambda w: pl.BlockSpec((tm, w), lambda i: (i, 0))
    gate_specs = [pl.BlockSpec((tm, SEG), functools.partial(lambda i, j: (i, j), j=SEG_GATE0 + j))
                  for j in range(6)]
    return pl.pallas_call(
        _merge_kernel,
        grid=(t // tm,),
        in_specs=[row_spec(D_MODEL), row_spec(SEG), row_spec(SEG), row_spec(SEG)]
        + gate_specs
        + [_of_layer((SEG, D_MODEL), l)] * 3
        + [_of_layer((D_MODEL, D_MODEL), l), _of_layer((1, D_MODEL), l), _of_layer((1, D_MODEL), l)],
        out_specs=row_spec(D_MODEL),
        out_shape=jax.ShapeDtypeStruct((t, D_MODEL), F32),
        compiler_params=_cparams(1),
        name="merge_out",
    )(x2, ret, sb, sg, *([proj] * 6), p_ret, p_sb, p_sgu, w_out, lng, lnb)


FF_CHUNK = 1024


def _mlp_kernel(x_ref, wu_ref, wd_ref, lng_ref, lnb_ref, o_ref):
    x = x_ref[...]
    xb = x.astype(BF16)
    acc = None
    for c in range(D_FF // FF_CHUNK):
        cols = slice(c * FF_CHUNK, (c + 1) * FF_CHUNK)
        h = jnp.maximum(jnp.dot(xb, wu_ref[:, cols], preferred_element_type=F32), 0.0)
        part = jnp.dot((h * h).astype(BF16), wd_ref[cols, :], preferred_element_type=F32)
        acc = part if acc is None else acc + part
    o_ref[...] = _layer_norm(DEEPNORM_ALPHA * x + acc, lng_ref[...], lnb_ref[...])


def _mlp(x2, w_up, w_down, lng, lnb, l, tm):
    t = x2.shape[0]
    return pl.pallas_call(
        _mlp_kernel,
        grid=(t // tm,),
        in_specs=[pl.BlockSpec((tm, D_MODEL), lambda i: (i, 0)),
                  _of_layer((D_MODEL, D_FF), l), _of_layer((D_FF, D_MODEL), l),
                  _of_layer((1, D_MODEL), l), _of_layer((1, D_MODEL), l)],
        out_specs=pl.BlockSpec((tm, D_MODEL), lambda i: (i, 0)),
        out_shape=jax.ShapeDtypeStruct((t, D_MODEL), F32),
        compiler_params=_cparams(1),
        name="mlp",
    )(x2, w_up, w_down, lng, lnb)


def _rotary_tables(seq):
    half = RET_HEAD_DIM // 2
    inv_freq = ROPE_BASE ** (-jnp.arange(half, dtype=F32) / half)
    ang = jnp.arange(seq, dtype=jnp.int32).astype(F32)[:, None] * inv_freq[None, :]
    cos, sin = jnp.cos(ang), jnp.sin(ang)
    return jnp.concatenate([cos, cos], axis=1), jnp.concatenate([-sin, sin], axis=1)


def _retention_tables():
    log_g = jnp.log(1.0 - 2.0 ** (-5.0 - jnp.arange(RET_HEADS, dtype=F32)))
    idx = jnp.arange(CHUNK, dtype=F32)
    diff = idx[:, None] - idx[None, :]
    dint = jnp.where(diff[None] >= 0, jnp.exp(log_g[:, None, None] * diff[None]), 0.0)
    k_decay = jnp.exp(log_g[:, None] * (CHUNK - 1 - idx)[None, :])
    q_decay = jnp.exp(log_g[:, None] * (idx + 1.0)[None, :])
    chunk_decay = jnp.exp(log_g * CHUNK)
    full = (RET_HEADS, CHUNK, RET_HEAD_DIM)
    return (dint.astype(F32),
            jnp.broadcast_to(q_decay[:, :, None], full),
            jnp.broadcast_to(k_decay[:, :, None], full),
            jnp.broadcast_to(chunk_decay[:, None, None], full))


def kernel(x, w_in, ret_gn_g, ret_gn_b, sgu_ln_g, sgu_ln_b, sgu_w, sgu_b, p_ret, p_sb, p_sgu,
           w_out, ln1_g, ln1_b, w_up, w_down, ln2_g, ln2_b):
    batch, seq, d = x.shape
    assert d == D_MODEL and seq % CHUNK == 0
    t = batch * seq
    tm_proj, tm_dense = 512, 1024
    assert seq % tm_proj == 0 and t % tm_dense == 0
    cs, sn = _rotary_tables(seq)
    ret_tabs = _retention_tables()
    bf = lambda a: a.astype(BF16)
    rows = lambda a: a[:, None, :]
    w_in, p_ret, p_sb, p_sgu, w_out, w_up, w_down = map(
        bf, (w_in, p_ret, p_sb, p_sgu, w_out, w_up, w_down))
    b_tab = jnp.broadcast_to(sgu_b[:, :, :, None], sgu_b.shape + (LANES,))
    ret_gn_g, ret_gn_b, sgu_ln_g, sgu_ln_b, ln1_g, ln1_b, ln2_g, ln2_b = map(
        rows, (ret_gn_g, ret_gn_b, sgu_ln_g, sgu_ln_b, ln1_g, ln1_b, ln2_g, ln2_b))
    x2 = x.reshape(t, d)
    for l in range(DEPTH):
        proj = _in_proj(x2, w_in, cs, sn, sgu_ln_g, sgu_ln_b, l, tm_proj, seq)
        ret = _retention(proj, ret_tabs, ret_gn_g, ret_gn_b, l, batch, seq)
        sb = _stick_breaking(proj, batch, seq)
        sg = _sgu(proj, sgu_w, b_tab, l, tm_proj)
        x2 = _merge(x2, ret, sb, sg, proj, p_ret, p_sb, p_sgu, w_out, ln1_g, ln1_b, l, tm_dense)
        x2 = _mlp(x2, w_up, w_down, ln2_g, ln2_b, l, tm_dense)
    return x2.reshape(batch, seq, d)
```

```python
import functools

import jax
import jax.numpy as jnp
from jax import lax
from jax.experimental import pallas as pl
from jax.experimental.pallas import tpu as pltpu

F32 = jnp.float32
BF16 = jnp.bfloat16

D_MODEL = 1024
DEPTH = 2
CHUNK = 128
RET_HEADS = 4
RET_HEAD_DIM = 128
SB_HEADS = 8
SB_HEAD_DIM = 64
SGU_GROUPS = 4
SEG = 512
D_FF = 4 * D_MODEL
N_IN = 9 * SEG + 3 * D_MODEL
ROPE_BASE = 10000.0
LN_EPS = 1e-5
DEEPNORM_ALPHA = (2 * DEPTH) ** 0.25

SEG_RQ, SEG_RK, SEG_RV, SEG_RG, SEG_SQ, SEG_SK, SEG_SV, SEG_GU, SEG_GV = range(9)
SEG_GATE0 = 9
N_SEG = N_IN // SEG

LANES = 128
VMEM_LIMIT = 56 * 1024 * 1024


def _cparams(n_axes):
    return pltpu.CompilerParams(
        dimension_semantics=("arbitrary",) * n_axes,
        vmem_limit_bytes=VMEM_LIMIT)


def _resident(shape):
    zeros = (0,) * len(shape)
    return pl.BlockSpec(shape, lambda *_: zeros, pipeline_mode=pl.Buffered(1))


def _of_layer(shape, l):
    zeros = (0,) * len(shape)
    return pl.BlockSpec((None,) + tuple(shape), lambda *_: (l,) + zeros,
                        pipeline_mode=pl.Buffered(1))


def _layer_norm(x, g, b):
    mu = jnp.mean(x, axis=-1, keepdims=True)
    xc = x - mu
    var = jnp.mean(xc * xc, axis=-1, keepdims=True)
    return xc * lax.rsqrt(var + LN_EPS) * g + b


def _inproj_kernel(x_ref, w_ref, cs_ref, sn_ref, lng_ref, lnb_ref, o_ref):
    xb = x_ref[...].astype(BF16)
    cs = cs_ref[...]
    sn = sn_ref[...]
    for j in range(N_SEG):
        cols = slice(j * SEG, (j + 1) * SEG)
        acc = jnp.dot(xb, w_ref[:, cols], preferred_element_type=F32)
        if j in (SEG_RQ, SEG_RK):
            scale = RET_HEAD_DIM ** -0.5 if j == SEG_RK else None
            for h in range(RET_HEADS):
                hc = slice(h * RET_HEAD_DIM, (h + 1) * RET_HEAD_DIM)
                xh = acc[:, hc]
                r = xh * cs + pltpu.roll(xh, RET_HEAD_DIM // 2, axis=1) * sn
                if scale is not None:
                    r = r * scale
                o_ref[:, j * SEG + h * RET_HEAD_DIM:j * SEG + (h + 1) * RET_HEAD_DIM] = r.astype(BF16)
            continue
        if j == SEG_RG:
            res = jax.nn.silu(acc)
        elif j == SEG_SQ:
            res = acc * (SB_HEAD_DIM ** -0.5 * LOG2E)
        elif j == SEG_GU:
            res = jax.nn.gelu(acc)
        elif j == SEG_GV:
            res = _layer_norm(jax.nn.gelu(acc), lng_ref[...], lnb_ref[...])
        elif j >= SEG_GATE0:
            res = jax.nn.sigmoid(acc)
        else:
            res = acc
        o_ref[:, cols] = res.astype(BF16)


def _in_proj(x2, w, cs, sn, lng, lnb, l, tm, seq):
    t = x2.shape[0]
    pos_blocks = seq // tm
    return pl.pallas_call(
        _inproj_kernel,
        grid=(t // tm,),
        in_specs=[
            pl.BlockSpec((tm, D_MODEL), lambda i: (i, 0)),
            _of_layer((D_MODEL, N_IN), l),
            pl.BlockSpec((tm, LANES), lambda i: (i % pos_blocks, 0)),
            pl.BlockSpec((tm, LANES), lambda i: (i % pos_blocks, 0)),
            _of_layer((1, SEG), l),
            _of_layer((1, SEG), l),
        ],
        out_specs=pl.BlockSpec((tm, N_IN), lambda i: (i, 0)),
        out_shape=jax.ShapeDtypeStruct((t, N_IN), BF16),
        compiler_params=_cparams(1),
        name="in_proj",
    )(x2, w, cs, sn, lng, lnb)


RET_ROWS = 1024
RET_UNROLL = 2


def _ret_kernel(q_ref, k_ref, v_ref, g_ref, dint_ref, qd_ref, kd_ref, cd_ref,
                gng_ref, gnb_ref, o_ref, state_ref):
    @pl.when(pl.program_id(1) == 0)
    def _():
        state_ref[...] = jnp.zeros_like(state_ref)

    heads = range(RET_HEADS)
    cols = [slice(h * RET_HEAD_DIM, (h + 1) * RET_HEAD_DIM) for h in heads]
    nt = (((1,), (1,)), ((), ()))
    tn = (((0,), (0,)), ((), ()))

    def body(n, carry):
        rows = [pl.ds(pl.multiple_of((RET_UNROLL * n + c) * CHUNK, CHUNK), CHUNK)
                for c in range(RET_UNROLL)]
        chains = [(c, h) for c in range(RET_UNROLL) for h in heads]
        q = {(c, h): q_ref[rows[c], cols[h]] for c, h in chains}
        k = {(c, h): k_ref[rows[c], cols[h]] for c, h in chains}
        v = {(c, h): v_ref[rows[c], cols[h]] for c, h in chains}
        s = {(c, h): lax.dot_general(q[c, h], k[c, h], nt, preferred_element_type=F32) * dint_ref[h]
             for c, h in chains}
        kdec = {(c, h): (k[c, h].astype(F32) * kd_ref[h]).astype(BF16) for c, h in chains}
        kv = {(c, h): lax.dot_general(kdec[c, h], v[c, h], tn, preferred_element_type=F32)
              for c, h in chains}
        qdec = {(c, h): (q[c, h].astype(F32) * qd_ref[h]).astype(BF16) for c, h in chains}
        st = {(0, h): state_ref[h] for h in heads}
        for c in range(RET_UNROLL):
            for h in heads:
                st[c + 1, h] = st[c, h] * cd_ref[h] + kv[c, h]
        ret = {(c, h): jnp.dot(jnp.concatenate([s[c, h].astype(BF16), qdec[c, h]], axis=1),
                               jnp.concatenate([v[c, h], st[c, h].astype(BF16)], axis=0),
                               preferred_element_type=F32)
               for c, h in chains}
        for h in heads:
            state_ref[h] = st[RET_UNROLL, h]
        for c, h in chains:
            y = _layer_norm(ret[c, h], gng_ref[:, cols[h]], gnb_ref[:, cols[h]])
            o_ref[rows[c], cols[h]] = (g_ref[rows[c], cols[h]].astype(F32) * y).astype(BF16)
        return carry

    lax.fori_loop(0, q_ref.shape[0] // (RET_UNROLL * CHUNK), body, 0)


def _retention(proj, tabs, gng, gnb, l, batch, seq):
    dint, qd, kd, cd = tabs
    t = proj.shape[0]
    steps = seq // RET_ROWS

    def seg_spec(seg):
        return pl.BlockSpec((RET_ROWS, SEG), lambda b, i: (b * steps + i, seg))

    tab_spec = _resident((RET_HEADS, CHUNK, RET_HEAD_DIM))
    return pl.pallas_call(
        _ret_kernel,
        grid=(batch, steps),
        in_specs=[seg_spec(SEG_RQ), seg_spec(SEG_RK), seg_spec(SEG_RV), seg_spec(SEG_RG),
                  tab_spec, tab_spec, tab_spec, tab_spec,
                  _of_layer((1, SEG), l), _of_layer((1, SEG), l)],
        out_specs=pl.BlockSpec((RET_ROWS, SEG), lambda b, i: (b * steps + i, 0)),
        out_shape=jax.ShapeDtypeStruct((t, SEG), BF16),
        scratch_shapes=[pltpu.VMEM((RET_HEADS, RET_HEAD_DIM, RET_HEAD_DIM), F32)],
        compiler_params=_cparams(2),
        name="retention",
    )(proj, proj, proj, proj, dint, qd, kd, cd, gng, gnb)


SB_Q = 256
SB_K = 256
SB_GROUPS = 2
LOG2E = 1.4426950408889634
SB_UNDERFLOW = 152.0
SB_MASKED = -1e30
SB_CLAMP = 126.0


def _sb_kernel(q_ref, k_ref, v_ref, o_ref):
    qi = pl.program_id(2)
    groups = range(SB_GROUPS)
    lanes = [slice(g * LANES, (g + 1) * LANES) for g in groups]
    low = lax.broadcasted_iota(jnp.int32, (SB_Q, LANES), 1) < SB_HEAD_DIM
    tr = lax.broadcasted_iota(jnp.int32, (SB_K, SB_K), 0)
    tc = lax.broadcasted_iota(jnp.int32, (SB_K, SB_K), 1)
    tri = jnp.where(tr >= tc, 1.0, 0.0).astype(BF16)

    def stacked_q(g):
        q = q_ref[:, lanes[g]]
        zero = jnp.zeros_like(q)
        return jnp.concatenate([jnp.where(low, q, zero), jnp.where(low, zero, q)], axis=0)

    qs = [stacked_q(g) for g in groups]

    def key_rows(kj):
        return pl.ds(pl.multiple_of(kj * SB_K, SB_K), SB_K)

    def scores(rows, g):
        return lax.dot_general(qs[g], k_ref[rows, lanes[g]], (((1,), (1,)), ((), ())),
                               preferred_element_type=F32)

    def suffix_sums(z):
        sp = jnp.maximum(jnp.log(1.0 + jnp.exp2(jnp.minimum(z, SB_CLAMP))) * LOG2E, z)
        return jnp.dot(sp.astype(BF16), tri, preferred_element_type=F32)

    def weighted(z, incl, run, rows, g):
        a = jnp.exp2(z - incl - run)
        return jnp.dot(a.astype(BF16), v_ref[rows, lanes[g]], preferred_element_type=F32)

    rows_d = key_rows(qi)
    rows_p = key_rows(jnp.maximum(qi - 1, 0))
    t_in = lax.broadcasted_iota(jnp.int32, (2 * SB_Q, SB_K), 0) & (SB_Q - 1)
    s_in = lax.broadcasted_iota(jnp.int32, (2 * SB_Q, SB_K), 1)
    causal = s_in < t_in
    z_d = [jnp.where(causal, scores(rows_d, g), SB_MASKED) for g in groups]
    z_p = [scores(rows_p, g) for g in groups]
    incl_d = [suffix_sums(z_d[g]) for g in groups]
    incl_p = [suffix_sums(z_p[g]) for g in groups]
    no_prev = jnp.where(qi == 0, -SB_MASKED, 0.0)
    run_p = [incl_d[g][:, 0:1] + no_prev for g in groups]
    acc = [weighted(z_d[g], incl_d[g], 0.0, rows_d, g)
           + weighted(z_p[g], incl_p[g], run_p[g], rows_p, g) for g in groups]
    run = [run_p[g] + incl_p[g][:, 0:1] for g in groups]

    def lowest(run):
        return functools.reduce(jnp.minimum, [jnp.min(r) for r in run])

    def live(c):
        return jnp.logical_and(c[0] < qi, c[1] <= SB_UNDERFLOW)

    def step(c):
        i, _, acc, run = c
        rows = key_rows(qi - 1 - i)
        z = [scores(rows, g) for g in groups]
        incl = [suffix_sums(z[g]) for g in groups]
        acc = [acc[g] + weighted(z[g], incl[g], run[g], rows, g) for g in groups]
        run = [run[g] + incl[g][:, 0:1] for g in groups]
        return i + 1, lowest(run), acc, run

    _, _, acc, _ = lax.while_loop(live, step, (jnp.int32(1), lowest(run), acc, run))
    for g in groups:
        o_ref[:, lanes[g]] = jnp.where(low, acc[g][:SB_Q], acc[g][SB_Q:]).astype(BF16)


def _stick_breaking(proj, batch, seq):
    assert SB_Q == SB_K and seq % SB_Q == 0
    t = proj.shape[0]
    nq = seq // SB_Q
    width = SB_GROUPS * LANES
    steps = SEG // width
    return pl.pallas_call(
        _sb_kernel,
        grid=(batch, steps, nq),
        in_specs=[
            pl.BlockSpec((SB_Q, width), lambda b, p, i: (b * nq + i, SEG_SQ * steps + p)),
            pl.BlockSpec((seq, width), lambda b, p, i: (b, SEG_SK * steps + p)),
            pl.BlockSpec((seq, width), lambda b, p, i: (b, SEG_SV * steps + p)),
        ],
        out_specs=pl.BlockSpec((SB_Q, width), lambda b, p, i: (b * nq + i, p)),
        out_shape=jax.ShapeDtypeStruct((t, SEG), BF16),
        compiler_params=_cparams(3),
        name="stick_breaking",
    )(proj, proj, proj)


FF_CHUNK = 1024


def _post_kernel(x_ref, ret_ref, sb_ref, gu_ref, gv_ref, g0, g1, g2, g3, g4, g5,
                 sw_ref, sbias_ref, pr_ref, ps_ref, pg_ref, wo_ref, ln1g_ref, ln1b_ref,
                 wu_ref, wd_ref, ln2g_ref, ln2b_ref, o_ref, sg_ref):
    row = lax.broadcasted_iota(jnp.int32, (CHUNK, CHUNK), 0)
    col = lax.broadcasted_iota(jnp.int32, (CHUNK, CHUNK), 1)
    causal = row >= col
    for g in range(SGU_GROUPS):
        wm = jnp.where(causal, sw_ref[g], 0.0).astype(BF16)
        bias = sbias_ref[g]
        gc = slice(g * LANES, (g + 1) * LANES)
        for c in range(x_ref.shape[0] // CHUNK):
            rc = slice(c * CHUNK, (c + 1) * CHUNK)
            sv = jnp.dot(wm, gv_ref[rc, gc], preferred_element_type=F32) + bias
            sg_ref[rc, gc] = (gu_ref[rc, gc].astype(F32) * sv).astype(BF16)

    gates = ((g0, g1), (g2, g3), (g4, g5))
    branches = ((ret_ref, pr_ref), (sb_ref, ps_ref), (sg_ref, pg_ref))
    halves = []
    for half in range(D_MODEL // SEG):
        cols = slice(half * SEG, (half + 1) * SEG)
        m = None
        for (a_ref, p_ref), gate in zip(branches, gates):
            term = gate[half][...].astype(F32) * jnp.dot(
                a_ref[...], p_ref[:, cols], preferred_element_type=F32)
            m = term if m is None else m + term
        halves.append(m.astype(BF16))
    y = (jnp.dot(halves[0], wo_ref[:SEG, :], preferred_element_type=F32)
         + jnp.dot(halves[1], wo_ref[SEG:, :], preferred_element_type=F32))
    x1 = _layer_norm(DEEPNORM_ALPHA * x_ref[...] + y, ln1g_ref[...], ln1b_ref[...])

    xb = x1.astype(BF16)
    acc = None
    for c in range(D_FF // FF_CHUNK):
        cols = slice(c * FF_CHUNK, (c + 1) * FF_CHUNK)
        h = jnp.maximum(jnp.dot(xb, wu_ref[:, cols], preferred_element_type=F32), 0.0)
        part = jnp.dot((h * h).astype(BF16), wd_ref[cols, :], preferred_element_type=F32)
        acc = part if acc is None else acc + part
    o_ref[...] = _layer_norm(DEEPNORM_ALPHA * x1 + acc, ln2g_ref[...], ln2b_ref[...])


def _post(x2, ret, sb, proj, sgu_w, b_tab, p_ret, p_sb, p_sgu, w_out, ln1g, ln1b,
          w_up, w_down, ln2g, ln2b, l, tm):
    t = x2.shape[0]
    row_spec = lambda w: pl.BlockSpec((tm, w), lambda i: (i, 0))
    seg_spec = lambda seg: pl.BlockSpec((tm, SEG), lambda i: (i, seg))
    return pl.pallas_call(
        _post_kernel,
        grid=(t // tm,),
        in_specs=[row_spec(D_MODEL), row_spec(SEG), row_spec(SEG),
                  seg_spec(SEG_GU), seg_spec(SEG_GV)]
        + [seg_spec(SEG_GATE0 + j) for j in range(6)]
        + [_of_layer((SGU_GROUPS, CHUNK, CHUNK), l), _of_layer((SGU_GROUPS, CHUNK, LANES), l)]
        + [_of_layer((SEG, D_MODEL), l)] * 3
        + [_of_layer((D_MODEL, D_MODEL), l), _of_layer((1, D_MODEL), l), _of_layer((1, D_MODEL), l),
           _of_layer((D_MODEL, D_FF), l), _of_layer((D_FF, D_MODEL), l),
           _of_layer((1, D_MODEL), l), _of_layer((1, D_MODEL), l)],
        out_specs=row_spec(D_MODEL),
        out_shape=jax.ShapeDtypeStruct((t, D_MODEL), F32),
        scratch_shapes=[pltpu.VMEM((tm, SEG), BF16)],
        compiler_params=_cparams(1),
        name="post_mixer",
    )(x2, ret, sb, *([proj] * 8), sgu_w, b_tab, p_ret, p_sb, p_sgu, w_out, ln1g, ln1b,
      w_up, w_down, ln2g, ln2b)


def _rotary_tables(seq):
    half = RET_HEAD_DIM // 2
    inv_freq = ROPE_BASE ** (-jnp.arange(half, dtype=F32) / half)
    ang = jnp.arange(seq, dtype=jnp.int32).astype(F32)[:, None] * inv_freq[None, :]
    cos, sin = jnp.cos(ang), jnp.sin(ang)
    return jnp.concatenate([cos, cos], axis=1), jnp.concatenate([-sin, sin], axis=1)


def _retention_tables():
    log_g = jnp.log(1.0 - 2.0 ** (-5.0 - jnp.arange(RET_HEADS, dtype=F32)))
    idx = jnp.arange(CHUNK, dtype=F32)
    diff = idx[:, None] - idx[None, :]
    dint = jnp.where(diff[None] >= 0, jnp.exp(log_g[:, None, None] * diff[None]), 0.0)
    k_decay = jnp.exp(log_g[:, None] * (CHUNK - 1 - idx)[None, :])
    q_decay = jnp.exp(log_g[:, None] * (idx + 1.0)[None, :])
    chunk_decay = jnp.exp(log_g * CHUNK)
    full = (RET_HEADS, CHUNK, RET_HEAD_DIM)
    return (dint.astype(F32),
            jnp.broadcast_to(q_decay[:, :, None], full),
            jnp.broadcast_to(k_decay[:, :, None], full),
            jnp.broadcast_to(chunk_decay[:, None, None], full))


def kernel(x, w_in, ret_gn_g, ret_gn_b, sgu_ln_g, sgu_ln_b, sgu_w, sgu_b, p_ret, p_sb, p_sgu,
           w_out, ln1_g, ln1_b, w_up, w_down, ln2_g, ln2_b):
    batch, seq, d = x.shape
    assert d == D_MODEL and seq % CHUNK == 0
    t = batch * seq
    tm = 512
    assert seq % tm == 0
    cs, sn = _rotary_tables(seq)
    ret_tabs = _retention_tables()
    bf = lambda a: a.astype(BF16)
    rows = lambda a: a[:, None, :]
    w_in, p_ret, p_sb, p_sgu, w_out, w_up, w_down = map(
        bf, (w_in, p_ret, p_sb, p_sgu, w_out, w_up, w_down))
    b_tab = jnp.broadcast_to(sgu_b[:, :, :, None], sgu_b.shape + (LANES,))
    ret_gn_g, ret_gn_b, sgu_ln_g, sgu_ln_b, ln1_g, ln1_b, ln2_g, ln2_b = map(
        rows, (ret_gn_g, ret_gn_b, sgu_ln_g, sgu_ln_b, ln1_g, ln1_b, ln2_g, ln2_b))
    x2 = x.reshape(t, d)
    for l in range(DEPTH):
        proj = _in_proj(x2, w_in, cs, sn, sgu_ln_g, sgu_ln_b, l, tm, seq)
        ret = _retention(proj, ret_tabs, ret_gn_g, ret_gn_b, l, batch, seq)
        sb = _stick_breaking(proj, batch, seq)
        x2 = _post(x2, ret, sb, proj, sgu_w, b_tab, p_ret, p_sb, p_sgu, w_out, ln1_g, ln1_b,
                   w_up, w_down, ln2_g, ln2_b, l, tm)
    return x2.reshape(batch, seq, d)
```

```python
import functools

import jax
import jax.numpy as jnp
from jax import lax
from jax.experimental import pallas as pl
from jax.experimental.pallas import tpu as pltpu

F32 = jnp.float32
BF16 = jnp.bfloat16

D_MODEL = 1024
DEPTH = 2
CHUNK = 128
RET_HEADS = 4
RET_HEAD_DIM = 128
SB_HEADS = 8
SB_HEAD_DIM = 64
SGU_GROUPS = 4
SEG = 512
D_FF = 4 * D_MODEL
N_IN = 9 * SEG + 3 * D_MODEL
ROPE_BASE = 10000.0
LN_EPS = 1e-5
DEEPNORM_ALPHA = (2 * DEPTH) ** 0.25

SEG_RQ, SEG_RK, SEG_RV, SEG_RG, SEG_SQ, SEG_SK, SEG_SV, SEG_GU, SEG_GV = range(9)
SEG_GATE0 = 9
N_SEG = N_IN // SEG

LANES = 128
BF16_SUBLANES = 16
VMEM_LIMIT = 56 * 1024 * 1024


def _cparams(n_axes):
    return pltpu.CompilerParams(
        dimension_semantics=("arbitrary",) * n_axes,
        vmem_limit_bytes=VMEM_LIMIT)


def _resident(shape):
    zeros = (0,) * len(shape)
    return pl.BlockSpec(shape, lambda *_: zeros, pipeline_mode=pl.Buffered(1))


def _of_layer(shape, l):
    zeros = (0,) * len(shape)
    return pl.BlockSpec((None,) + tuple(shape), lambda *_: (l,) + zeros,
                        pipeline_mode=pl.Buffered(1))


def _cast_rider(params, l, n_steps):
    in_specs, out_specs, out_shapes = [], [], []
    for p in params:
        _, r, c = p.shape
        assert r % (n_steps * BF16_SUBLANES) == 0
        rb = r // n_steps
        in_specs.append(pl.BlockSpec((None, rb, c), lambda i: (l, i, 0)))
        out_specs.append(pl.BlockSpec((rb, c), lambda i: (i, 0)))
        out_shapes.append(jax.ShapeDtypeStruct((r, c), BF16))
    return in_specs, out_specs, out_shapes


def _run_casts(src_refs, dst_refs):
    for src, dst in zip(src_refs, dst_refs):
        dst[...] = src[...].astype(BF16)


def _layer_norm(x, g, b):
    mu = jnp.mean(x, axis=-1, keepdims=True)
    xc = x - mu
    var = jnp.mean(xc * xc, axis=-1, keepdims=True)
    return xc * lax.rsqrt(var + LN_EPS) * g + b


def _inproj_kernel(n_casts, x_ref, w_ref, cs_ref, sn_ref, lng_ref, lnb_ref, *refs):
    o_ref = refs[n_casts]
    _run_casts(refs[:n_casts], refs[n_casts + 1:])
    xb = x_ref[...].astype(BF16)
    cs = cs_ref[...]
    sn = sn_ref[...]
    for j in range(N_SEG):
        cols = slice(j * SEG, (j + 1) * SEG)
        acc = jnp.dot(xb, w_ref[:, cols], preferred_element_type=F32)
        if j in (SEG_RQ, SEG_RK):
            scale = RET_HEAD_DIM ** -0.5 if j == SEG_RK else None
            for h in range(RET_HEADS):
                hc = slice(h * RET_HEAD_DIM, (h + 1) * RET_HEAD_DIM)
                xh = acc[:, hc]
                r = xh * cs + pltpu.roll(xh, RET_HEAD_DIM // 2, axis=1) * sn
                if scale is not None:
                    r = r * scale
                o_ref[:, j * SEG + h * RET_HEAD_DIM:j * SEG + (h + 1) * RET_HEAD_DIM] = r.astype(BF16)
            continue
        if j == SEG_RG:
            res = jax.nn.silu(acc)
        elif j == SEG_SQ:
            res = acc * (SB_HEAD_DIM ** -0.5 * LOG2E)
        elif j == SEG_GU:
            res = jax.nn.gelu(acc)
        elif j == SEG_GV:
            res = _layer_norm(jax.nn.gelu(acc), lng_ref[...], lnb_ref[...])
        elif j >= SEG_GATE0:
            res = jax.nn.sigmoid(acc)
        else:
            res = acc
        o_ref[:, cols] = res.astype(BF16)


def _in_proj(x2, w, cs, sn, lng, lnb, to_cast, l, tm, seq):
    t = x2.shape[0]
    pos_blocks = seq // tm
    cast_in, cast_out, cast_shapes = _cast_rider(to_cast, l, t // tm)
    out = pl.pallas_call(
        functools.partial(_inproj_kernel, len(to_cast)),
        grid=(t // tm,),
        in_specs=[
            pl.BlockSpec((tm, D_MODEL), lambda i: (i, 0)),
            _resident((D_MODEL, N_IN)),
            pl.BlockSpec((tm, LANES), lambda i: (i % pos_blocks, 0)),
            pl.BlockSpec((tm, LANES), lambda i: (i % pos_blocks, 0)),
            _of_layer((1, SEG), l),
            _of_layer((1, SEG), l),
        ] + cast_in,
        out_specs=[pl.BlockSpec((tm, N_IN), lambda i: (i, 0))] + cast_out,
        out_shape=[jax.ShapeDtypeStruct((t, N_IN), BF16)] + cast_shapes,
        compiler_params=_cparams(1),
        name="in_proj",
    )(x2, w, cs, sn, lng, lnb, *to_cast)
    return out[0], out[1:]


RET_ROWS = 1024
RET_UNROLL = 4


def _ret_kernel(q_ref, k_ref, v_ref, g_ref, dint_ref, qd_ref, kd_ref, cd_ref,
                gng_ref, gnb_ref, o_ref, state_ref):
    @pl.when(pl.program_id(1) == 0)
    def _():
        state_ref[...] = jnp.zeros_like(state_ref)

    heads = range(RET_HEADS)
    cols = [slice(h * RET_HEAD_DIM, (h + 1) * RET_HEAD_DIM) for h in heads]
    nt = (((1,), (1,)), ((), ()))
    tn = (((0,), (0,)), ((), ()))

    def body(n, carry):
        rows = [pl.ds(pl.multiple_of((RET_UNROLL * n + c) * CHUNK, CHUNK), CHUNK)
                for c in range(RET_UNROLL)]
        chains = [(c, h) for c in range(RET_UNROLL) for h in heads]
        q = {(c, h): q_ref[rows[c], cols[h]] for c, h in chains}
        k = {(c, h): k_ref[rows[c], cols[h]] for c, h in chains}
        v = {(c, h): v_ref[rows[c], cols[h]] for c, h in chains}
        s = {(c, h): lax.dot_general(q[c, h], k[c, h], nt, preferred_element_type=F32) * dint_ref[h]
             for c, h in chains}
        kdec = {(c, h): (k[c, h].astype(F32) * kd_ref[h]).astype(BF16) for c, h in chains}
        kv = {(c, h): lax.dot_general(kdec[c, h], v[c, h], tn, preferred_element_type=F32)
              for c, h in chains}
        qdec = {(c, h): (q[c, h].astype(F32) * qd_ref[h]).astype(BF16) for c, h in chains}
        st = {(0, h): state_ref[h] for h in heads}
        for c in range(RET_UNROLL):
            for h in heads:
                st[c + 1, h] = st[c, h] * cd_ref[h] + kv[c, h]
        ret = {(c, h): jnp.dot(jnp.concatenate([s[c, h].astype(BF16), qdec[c, h]], axis=1),
                               jnp.concatenate([v[c, h], st[c, h].astype(BF16)], axis=0),
                               preferred_element_type=F32)
               for c, h in chains}
        for h in heads:
            state_ref[h] = st[RET_UNROLL, h]
        for c, h in chains:
            y = _layer_norm(ret[c, h], gng_ref[:, cols[h]], gnb_ref[:, cols[h]])
            o_ref[rows[c], cols[h]] = (g_ref[rows[c], cols[h]].astype(F32) * y).astype(BF16)
        return carry

    lax.fori_loop(0, q_ref.shape[0] // (RET_UNROLL * CHUNK), body, 0)


def _retention(proj, tabs, gng, gnb, l, batch, seq):
    dint, qd, kd, cd = tabs
    t = proj.shape[0]
    steps = seq // RET_ROWS

    def seg_spec(seg):
        return pl.BlockSpec((RET_ROWS, SEG), lambda b, i: (b * steps + i, seg))

    tab_spec = _resident((RET_HEADS, CHUNK, RET_HEAD_DIM))
    return pl.pallas_call(
        _ret_kernel,
        grid=(batch, steps),
        in_specs=[seg_spec(SEG_RQ), seg_spec(SEG_RK), seg_spec(SEG_RV), seg_spec(SEG_RG),
                  tab_spec, tab_spec, tab_spec, tab_spec,
                  _of_layer((1, SEG), l), _of_layer((1, SEG), l)],
        out_specs=pl.BlockSpec((RET_ROWS, SEG), lambda b, i: (b * steps + i, 0)),
        out_shape=jax.ShapeDtypeStruct((t, SEG), BF16),
        scratch_shapes=[pltpu.VMEM((RET_HEADS, RET_HEAD_DIM, RET_HEAD_DIM), F32)],
        compiler_params=_cparams(2),
        name="retention",
    )(proj, proj, proj, proj, dint, qd, kd, cd, gng, gnb)


SB_Q = 256
SB_K = 256
SB_GROUPS = 2
LOG2E = 1.4426950408889634
SB_UNDERFLOW = 152.0
SB_MASKED = -1e30
SB_CLAMP = 126.0


def _sb_kernel(q_ref, k_ref, v_ref, o_ref):
    qi = pl.program_id(2)
    groups = range(SB_GROUPS)
    lanes = [slice(g * LANES, (g + 1) * LANES) for g in groups]
    low = lax.broadcasted_iota(jnp.int32, (SB_Q, LANES), 1) < SB_HEAD_DIM
    tr = lax.broadcasted_iota(jnp.int32, (SB_K, SB_K), 0)
    tc = lax.broadcasted_iota(jnp.int32, (SB_K, SB_K), 1)
    tri = jnp.where(tr >= tc, 1.0, 0.0).astype(BF16)

    def stacked_q(g):
        q = q_ref[:, lanes[g]]
        zero = jnp.zeros_like(q)
        return jnp.concatenate([jnp.where(low, q, zero), jnp.where(low, zero, q)], axis=0)

    qs = [stacked_q(g) for g in groups]

    def key_rows(kj):
        return pl.ds(pl.multiple_of(kj * SB_K, SB_K), SB_K)

    def scores(rows, g):
        return lax.dot_general(qs[g], k_ref[rows, lanes[g]], (((1,), (1,)), ((), ())),
                               preferred_element_type=F32)

    def suffix_sums(z):
        sp = jnp.maximum(jnp.log(1.0 + jnp.exp2(jnp.minimum(z, SB_CLAMP))) * LOG2E, z)
        return jnp.dot(sp.astype(BF16), tri, preferred_element_type=F32)

    def weighted(z, incl, run, rows, g):
        a = jnp.exp2(z - incl - run)
        return jnp.dot(a.astype(BF16), v_ref[rows, lanes[g]], preferred_element_type=F32)

    rows_d = key_rows(qi)
    rows_p = key_rows(jnp.maximum(qi - 1, 0))
    t_in = lax.broadcasted_iota(jnp.int32, (2 * SB_Q, SB_K), 0) & (SB_Q - 1)
    s_in = lax.broadcasted_iota(jnp.int32, (2 * SB_Q, SB_K), 1)
    causal = s_in < t_in
    z_d = [jnp.where(causal, scores(rows_d, g), SB_MASKED) for g in groups]
    z_p = [scores(rows_p, g) for g in groups]
    incl_d = [suffix_sums(z_d[g]) for g in groups]
    incl_p = [suffix_sums(z_p[g]) for g in groups]
    no_prev = jnp.where(qi == 0, -SB_MASKED, 0.0)
    run_p = [incl_d[g][:, 0:1] + no_prev for g in groups]
    acc = [weighted(z_d[g], incl_d[g], 0.0, rows_d, g)
           + weighted(z_p[g], incl_p[g], run_p[g], rows_p, g) for g in groups]
    run = [run_p[g] + incl_p[g][:, 0:1] for g in groups]

    def lowest(run):
        return functools.reduce(jnp.minimum, [jnp.min(r) for r in run])

    def live(c):
        return jnp.logical_and(c[0] < qi, c[1] <= SB_UNDERFLOW)

    def step(c):
        i, _, acc, run = c
        rows = key_rows(qi - 1 - i)
        z = [scores(rows, g) for g in groups]
        incl = [suffix_sums(z[g]) for g in groups]
        acc = [acc[g] + weighted(z[g], incl[g], run[g], rows, g) for g in groups]
        run = [run[g] + incl[g][:, 0:1] for g in groups]
        return i + 1, lowest(run), acc, run

    _, _, acc, _ = lax.while_loop(live, step, (jnp.int32(1), lowest(run), acc, run))
    for g in groups:
        o_ref[:, lanes[g]] = jnp.where(low, acc[g][:SB_Q], acc[g][SB_Q:]).astype(BF16)


def _stick_breaking(proj, batch, seq):
    assert SB_Q == SB_K and seq % SB_Q == 0
    t = proj.shape[0]
    nq = seq // SB_Q
    width = SB_GROUPS * LANES
    steps = SEG // width
    return pl.pallas_call(
        _sb_kernel,
        grid=(batch, steps, nq),
        in_specs=[
            pl.BlockSpec((SB_Q, width), lambda b, p, i: (b * nq + i, SEG_SQ * steps + p)),
            pl.BlockSpec((seq, width), lambda b, p, i: (b, SEG_SK * steps + p)),
            pl.BlockSpec((seq, width), lambda b, p, i: (b, SEG_SV * steps + p)),
        ],
        out_specs=pl.BlockSpec((SB_Q, width), lambda b, p, i: (b * nq + i, p)),
        out_shape=jax.ShapeDtypeStruct((t, SEG), BF16),
        compiler_params=_cparams(3),
        name="stick_breaking",
    )(proj, proj, proj)


FF_CHUNK = 1024


def _post_kernel(n_casts, x_ref, ret_ref, sb_ref, gu_ref, gv_ref, g0, g1, g2, g3, g4, g5,
                 sw_ref, sbias_ref, pr_ref, ps_ref, pg_ref, wo_ref, ln1g_ref, ln1b_ref,
                 wu_ref, wd_ref, ln2g_ref, ln2b_ref, *refs):
    o_ref = refs[n_casts]
    sg_ref = refs[-1]
    _run_casts(refs[:n_casts], refs[n_casts + 1:-1])
    row = lax.broadcasted_iota(jnp.int32, (CHUNK, CHUNK), 0)
    col = lax.broadcasted_iota(jnp.int32, (CHUNK, CHUNK), 1)
    causal = row >= col
    for g in range(SGU_GROUPS):
        wm = jnp.where(causal, sw_ref[g], 0.0).astype(BF16)
        bias = sbias_ref[g]
        gc = slice(g * LANES, (g + 1) * LANES)
        for c in range(x_ref.shape[0] // CHUNK):
            rc = slice(c * CHUNK, (c + 1) * CHUNK)
            sv = jnp.dot(wm, gv_ref[rc, gc], preferred_element_type=F32) + bias
            sg_ref[rc, gc] = (gu_ref[rc, gc].astype(F32) * sv).astype(BF16)

    gates = ((g0, g1), (g2, g3), (g4, g5))
    branches = ((ret_ref, pr_ref), (sb_ref, ps_ref), (sg_ref, pg_ref))
    halves = []
    for half in range(D_MODEL // SEG):
        cols = slice(half * SEG, (half + 1) * SEG)
        m = None
        for (a_ref, p_ref), gate in zip(branches, gates):
            term = gate[half][...].astype(F32) * jnp.dot(
                a_ref[...], p_ref[:, cols], preferred_element_type=F32)
            m = term if m is None else m + term
        halves.append(m.astype(BF16))
    y = (jnp.dot(halves[0], wo_ref[:SEG, :], preferred_element_type=F32)
         + jnp.dot(halves[1], wo_ref[SEG:, :], preferred_element_type=F32))
    x1 = _layer_norm(DEEPNORM_ALPHA * x_ref[...] + y, ln1g_ref[...], ln1b_ref[...])

    xb = x1.astype(BF16)
    acc = None
    for c in range(D_FF // FF_CHUNK):
        cols = slice(c * FF_CHUNK, (c + 1) * FF_CHUNK)
        h = jnp.maximum(jnp.dot(xb, wu_ref[:, cols], preferred_element_type=F32), 0.0)
        part = jnp.dot((h * h).astype(BF16), wd_ref[cols, :], preferred_element_type=F32)
        acc = part if acc is None else acc + part
    o_ref[...] = _layer_norm(DEEPNORM_ALPHA * x1 + acc, ln2g_ref[...], ln2b_ref[...])


def _post(x2, ret, sb, proj, sgu_w, b_tab, weights, ln1g, ln1b, ln2g, ln2b, to_cast, l, tm):
    t = x2.shape[0]
    p_ret, p_sb, p_sgu, w_out, w_up, w_down = weights
    row_spec = lambda w: pl.BlockSpec((tm, w), lambda i: (i, 0))
    seg_spec = lambda seg: pl.BlockSpec((tm, SEG), lambda i: (i, seg))
    cast_in, cast_out, cast_shapes = _cast_rider(to_cast, l + 1, t // tm)
    out = pl.pallas_call(
        functools.partial(_post_kernel, len(to_cast)),
        grid=(t // tm,),
        in_specs=[row_spec(D_MODEL), row_spec(SEG), row_spec(SEG),
                  seg_spec(SEG_GU), seg_spec(SEG_GV)]
        + [seg_spec(SEG_GATE0 + j) for j in range(6)]
        + [_of_layer((SGU_GROUPS, CHUNK, CHUNK), l), _of_layer((SGU_GROUPS, CHUNK, LANES), l)]
        + [_resident((SEG, D_MODEL))] * 3
        + [_resident((D_MODEL, D_MODEL)), _of_layer((1, D_MODEL), l), _of_layer((1, D_MODEL), l),
           _resident((D_MODEL, D_FF)), _resident((D_FF, D_MODEL)),
           _of_layer((1, D_MODEL), l), _of_layer((1, D_MODEL), l)]
        + cast_in,
        out_specs=[row_spec(D_MODEL)] + cast_out,
        out_shape=[jax.ShapeDtypeStruct((t, D_MODEL), F32)] + cast_shapes,
        scratch_shapes=[pltpu.VMEM((tm, SEG), BF16)],
        compiler_params=_cparams(1),
        name="post_mixer",
    )(x2, ret, sb, *([proj] * 8), sgu_w, b_tab, p_ret, p_sb, p_sgu, w_out, ln1g, ln1b,
      w_up, w_down, ln2g, ln2b, *to_cast)
    return out[0], out[1:]


def _rotary_tables(seq):
    half = RET_HEAD_DIM // 2
    inv_freq = ROPE_BASE ** (-jnp.arange(half, dtype=F32) / half)
    ang = jnp.arange(seq, dtype=jnp.int32).astype(F32)[:, None] * inv_freq[None, :]
    cos, sin = jnp.cos(ang), jnp.sin(ang)
    return jnp.concatenate([cos, cos], axis=1), jnp.concatenate([-sin, sin], axis=1)


def _retention_tables():
    log_g = jnp.log(1.0 - 2.0 ** (-5.0 - jnp.arange(RET_HEADS, dtype=F32)))
    idx = jnp.arange(CHUNK, dtype=F32)
    diff = idx[:, None] - idx[None, :]
    dint = jnp.where(diff[None] >= 0, jnp.exp(log_g[:, None, None] * diff[None]), 0.0)
    k_decay = jnp.exp(log_g[:, None] * (CHUNK - 1 - idx)[None, :])
    q_decay = jnp.exp(log_g[:, None] * (idx + 1.0)[None, :])
    chunk_decay = jnp.exp(log_g * CHUNK)
    full = (RET_HEADS, CHUNK, RET_HEAD_DIM)
    return (dint.astype(F32),
            jnp.broadcast_to(q_decay[:, :, None], full),
            jnp.broadcast_to(k_decay[:, :, None], full),
            jnp.broadcast_to(chunk_decay[:, None, None], full))


def kernel(x, w_in, ret_gn_g, ret_gn_b, sgu_ln_g, sgu_ln_b, sgu_w, sgu_b, p_ret, p_sb, p_sgu,
           w_out, ln1_g, ln1_b, w_up, w_down, ln2_g, ln2_b):
    batch, seq, d = x.shape
    assert d == D_MODEL and seq % CHUNK == 0
    t = batch * seq
    tm = 512
    assert seq % tm == 0
    cs, sn = _rotary_tables(seq)
    ret_tabs = _retention_tables()
    rows = lambda a: a[:, None, :]
    b_tab = jnp.broadcast_to(sgu_b[:, :, :, None], sgu_b.shape + (LANES,))
    ret_gn_g, ret_gn_b, sgu_ln_g, sgu_ln_b, ln1_g, ln1_b, ln2_g, ln2_b = map(
        rows, (ret_gn_g, ret_gn_b, sgu_ln_g, sgu_ln_b, ln1_g, ln1_b, ln2_g, ln2_b))
    post_weights = (p_ret, p_sb, p_sgu, w_out, w_up, w_down)
    x2 = x.reshape(t, d)
    w_proj = w_in[0].astype(BF16)
    for l in range(DEPTH):
        proj, weights = _in_proj(x2, w_proj, cs, sn, sgu_ln_g, sgu_ln_b, post_weights, l, tm, seq)
        ret = _retention(proj, ret_tabs, ret_gn_g, ret_gn_b, l, batch, seq)
        sb = _stick_breaking(proj, batch, seq)
        next_proj = (w_in,) if l + 1 < DEPTH else ()
        x2, cast = _post(x2, ret, sb, proj, sgu_w, b_tab, weights, ln1_g, ln1_b, ln2_g, ln2_b,
                         next_proj, l, tm)
        if cast:
            w_proj = cast[0]
    return x2.reshape(batch, seq, d)
```

```python
import functools

import jax
import jax.numpy as jnp
import numpy as np
from jax import lax
from jax.experimental import pallas as pl
from jax.experimental.pallas import tpu as pltpu

F32 = jnp.float32
BF16 = jnp.bfloat16

D_MODEL = 1024
DEPTH = 2
CHUNK = 128
RET_HEADS = 4
RET_HEAD_DIM = 128
SB_HEADS = 8
SB_HEAD_DIM = 64
SGU_GROUPS = 4
SEG = 512
D_FF = 4 * D_MODEL
N_IN = 9 * SEG + 3 * D_MODEL
ROPE_BASE = 10000.0
LN_EPS = 1e-5
DEEPNORM_ALPHA = (2 * DEPTH) ** 0.25

SEG_RQ, SEG_RK, SEG_RV, SEG_RG, SEG_SQ, SEG_SK, SEG_SV, SEG_GU, SEG_GV = range(9)
SEG_GATE0 = 9
N_SEG = N_IN // SEG
PROJ_SEG0 = SEG_SQ
N_PROJ = (N_SEG - PROJ_SEG0) * SEG

LANES = 128
BF16_SUBLANES = 16
VMEM_LIMIT = 56 * 1024 * 1024


def _cparams(n_axes):
    return pltpu.CompilerParams(
        dimension_semantics=("arbitrary",) * n_axes,
        vmem_limit_bytes=VMEM_LIMIT)


def _resident(shape):
    zeros = (0,) * len(shape)
    return pl.BlockSpec(shape, lambda *_: zeros, pipeline_mode=pl.Buffered(1))


def _of_layer(shape, l):
    zeros = (0,) * len(shape)
    return pl.BlockSpec((None,) + tuple(shape), lambda *_: (l,) + zeros,
                        pipeline_mode=pl.Buffered(1))


def _cast_rider(params, l, n_steps):
    in_specs, out_specs, out_shapes = [], [], []
    for p in params:
        _, r, c = p.shape
        assert r % (n_steps * BF16_SUBLANES) == 0
        rb = r // n_steps
        in_specs.append(pl.BlockSpec((None, rb, c), lambda i: (l, i, 0)))
        out_specs.append(pl.BlockSpec((rb, c), lambda i: (i, 0)))
        out_shapes.append(jax.ShapeDtypeStruct((r, c), BF16))
    return in_specs, out_specs, out_shapes


def _run_casts(src_refs, dst_refs):
    for src, dst in zip(src_refs, dst_refs):
        dst[...] = src[...].astype(BF16)


def _layer_norm(x, g, b):
    mu = jnp.mean(x, axis=-1, keepdims=True)
    xc = x - mu
    var = jnp.mean(xc * xc, axis=-1, keepdims=True)
    return xc * lax.rsqrt(var + LN_EPS) * g + b


def _inproj_kernel(n_casts, steps_per_seq, x_ref, w_ref, cs_ref, sn_ref, lng_ref, lnb_ref,
                   dint_ref, qd_ref, kd_ref, cd_ref, gng_ref, gnb_ref, *refs):
    o_ref, ret_ref = refs[n_casts], refs[n_casts + 1]
    state_ref = refs[-1]
    _run_casts(refs[:n_casts], refs[n_casts + 2:-1])

    @pl.when(pl.program_id(0) % steps_per_seq == 0)
    def _():
        state_ref[...] = jnp.zeros_like(state_ref)

    xb = x_ref[...].astype(BF16)
    cs = cs_ref[...]
    sn = sn_ref[...]
    heads = range(RET_HEADS)
    n_chunks = x_ref.shape[0] // CHUNK
    chains = [(c, h) for c in range(n_chunks) for h in heads]
    hcols = [slice(h * RET_HEAD_DIM, (h + 1) * RET_HEAD_DIM) for h in heads]
    crows = [slice(c * CHUNK, (c + 1) * CHUNK) for c in range(n_chunks)]
    nt = (((1,), (1,)), ((), ()))
    tn = (((0,), (0,)), ((), ()))

    def project(j):
        return jnp.dot(xb, w_ref[:, j * SEG:(j + 1) * SEG], preferred_element_type=F32)

    def rotary(acc, scale):
        out = []
        for h in heads:
            xh = acc[:, hcols[h]]
            r = xh * cs + pltpu.roll(xh, RET_HEAD_DIM // 2, axis=1) * sn
            out.append((r * scale).astype(BF16) if scale is not None else r.astype(BF16))
        return out

    def emit(j):
        acc = project(j)
        if j == SEG_SQ:
            res = acc * (SB_HEAD_DIM ** -0.5 * LOG2E)
        elif j == SEG_GU:
            res = jax.nn.gelu(acc)
        elif j == SEG_GV:
            res = _layer_norm(jax.nn.gelu(acc), lng_ref[...], lnb_ref[...])
        elif j >= SEG_GATE0:
            res = jax.nn.sigmoid(acc)
        else:
            res = acc
        o_ref[:, (j - PROJ_SEG0) * SEG:(j - PROJ_SEG0 + 1) * SEG] = res.astype(BF16)

    rq = rotary(project(SEG_RQ), None)
    rk = rotary(project(SEG_RK), RET_HEAD_DIM ** -0.5)
    rv = project(SEG_RV).astype(BF16)
    rg = jax.nn.silu(project(SEG_RG)).astype(BF16)
    q = {(c, h): rq[h][crows[c]] for c, h in chains}
    k = {(c, h): rk[h][crows[c]] for c, h in chains}
    v = {(c, h): rv[crows[c], hcols[h]] for c, h in chains}
    s = {(c, h): lax.dot_general(q[c, h], k[c, h], nt, preferred_element_type=F32) * dint_ref[h]
         for c, h in chains}
    kdec = {(c, h): (k[c, h].astype(F32) * kd_ref[h]).astype(BF16) for c, h in chains}
    kv = {(c, h): lax.dot_general(kdec[c, h], v[c, h], tn, preferred_element_type=F32)
          for c, h in chains}
    qdec = {(c, h): (q[c, h].astype(F32) * qd_ref[h]).astype(BF16) for c, h in chains}
    emit(SEG_SQ)
    emit(SEG_SK)
    st = {(0, h): state_ref[h] for h in heads}
    for c in range(n_chunks):
        for h in heads:
            st[c + 1, h] = st[c, h] * cd_ref[h] + kv[c, h]
    for h in heads:
        state_ref[h] = st[n_chunks, h]
    ret = {(c, h): jnp.dot(jnp.concatenate([s[c, h].astype(BF16), qdec[c, h]], axis=1),
                           jnp.concatenate([v[c, h], st[c, h].astype(BF16)], axis=0),
                           preferred_element_type=F32)
           for c, h in chains}
    emit(SEG_SV)
    emit(SEG_GU)
    for c, h in chains:
        y = _layer_norm(ret[c, h], gng_ref[:, hcols[h]], gnb_ref[:, hcols[h]])
        ret_ref[crows[c], hcols[h]] = (rg[crows[c], hcols[h]].astype(F32) * y).astype(BF16)
    for j in range(SEG_GV, N_SEG):
        emit(j)


def _in_proj(x2, w, cs, sn, lng, lnb, ret_tabs, gng, gnb, to_cast, l, tm, seq):
    t = x2.shape[0]
    pos_blocks = seq // tm
    cast_in, cast_out, cast_shapes = _cast_rider(to_cast, l, t // tm)
    tab_spec = _resident((RET_HEADS, CHUNK, RET_HEAD_DIM))
    out = pl.pallas_call(
        functools.partial(_inproj_kernel, len(to_cast), pos_blocks),
        grid=(t // tm,),
        in_specs=[
            pl.BlockSpec((tm, D_MODEL), lambda i: (i, 0)),
            _resident((D_MODEL, N_IN)),
            pl.BlockSpec((tm, LANES), lambda i: (i % pos_blocks, 0)),
            pl.BlockSpec((tm, LANES), lambda i: (i % pos_blocks, 0)),
            _of_layer((1, SEG), l),
            _of_layer((1, SEG), l),
            tab_spec, tab_spec, tab_spec, tab_spec,
            _of_layer((1, SEG), l),
            _of_layer((1, SEG), l),
        ] + cast_in,
        out_specs=[pl.BlockSpec((tm, N_PROJ), lambda i: (i, 0)),
                   pl.BlockSpec((tm, SEG), lambda i: (i, 0))] + cast_out,
        out_shape=[jax.ShapeDtypeStruct((t, N_PROJ), BF16),
                   jax.ShapeDtypeStruct((t, SEG), BF16)] + cast_shapes,
        scratch_shapes=[pltpu.VMEM((RET_HEADS, RET_HEAD_DIM, RET_HEAD_DIM), F32)],
        compiler_params=_cparams(1),
        name="in_proj",
    )(x2, w, cs, sn, lng, lnb, *ret_tabs, gng, gnb, *to_cast)
    return out[0], out[1], out[2:]


SB_Q = 256
SB_K = 256
SB_GROUPS = 2
LOG2E = 1.4426950408889634
SB_UNDERFLOW = 152.0
SB_MASKED = -1e30
SB_CLAMP = 126.0


def _sb_kernel(q_ref, k_ref, v_ref, o_ref):
    qi = pl.program_id(2)
    groups = range(SB_GROUPS)
    lanes = [slice(g * LANES, (g + 1) * LANES) for g in groups]
    low = lax.broadcasted_iota(jnp.int32, (SB_Q, LANES), 1) < SB_HEAD_DIM
    tr = lax.broadcasted_iota(jnp.int32, (SB_K, SB_K), 0)
    tc = lax.broadcasted_iota(jnp.int32, (SB_K, SB_K), 1)
    tri = jnp.where(tr >= tc, 1.0, 0.0).astype(BF16)

    def stacked_q(g):
        q = q_ref[:, lanes[g]]
        zero = jnp.zeros_like(q)
        return jnp.concatenate([jnp.where(low, q, zero), jnp.where(low, zero, q)], axis=0)

    qs = [stacked_q(g) for g in groups]

    def key_rows(kj):
        return pl.ds(pl.multiple_of(kj * SB_K, SB_K), SB_K)

    def scores(rows, g):
        return lax.dot_general(qs[g], k_ref[rows, lanes[g]], (((1,), (1,)), ((), ())),
                               preferred_element_type=F32)

    def suffix_sums(z):
        sp = jnp.maximum(jnp.log(1.0 + jnp.exp2(jnp.minimum(z, SB_CLAMP))) * LOG2E, z)
        return jnp.dot(sp.astype(BF16), tri, preferred_element_type=F32)

    def weighted(z, incl, run, rows, g):
        a = jnp.exp2(z - incl - run)
        return jnp.dot(a.astype(BF16), v_ref[rows, lanes[g]], preferred_element_type=F32)

    rows_d = key_rows(qi)
    rows_p = key_rows(jnp.maximum(qi - 1, 0))
    t_in = lax.broadcasted_iota(jnp.int32, (2 * SB_Q, SB_K), 0) & (SB_Q - 1)
    s_in = lax.broadcasted_iota(jnp.int32, (2 * SB_Q, SB_K), 1)
    causal = s_in < t_in
    z_d = [jnp.where(causal, scores(rows_d, g), SB_MASKED) for g in groups]
    z_p = [scores(rows_p, g) for g in groups]
    incl_d = [suffix_sums(z_d[g]) for g in groups]
    incl_p = [suffix_sums(z_p[g]) for g in groups]
    no_prev = jnp.where(qi == 0, -SB_MASKED, 0.0)
    run_p = [incl_d[g][:, 0:1] + no_prev for g in groups]
    acc = [weighted(z_d[g], incl_d[g], 0.0, rows_d, g)
           + weighted(z_p[g], incl_p[g], run_p[g], rows_p, g) for g in groups]
    run = [run_p[g] + incl_p[g][:, 0:1] for g in groups]

    def lowest(run):
        return functools.reduce(jnp.minimum, [jnp.min(r) for r in run])

    def live(c):
        return jnp.logical_and(c[0] < qi, c[1] <= SB_UNDERFLOW)

    def step(c):
        i, _, acc, run = c
        rows = key_rows(qi - 1 - i)
        z = [scores(rows, g) for g in groups]
        incl = [suffix_sums(z[g]) for g in groups]
        acc = [acc[g] + weighted(z[g], incl[g], run[g], rows, g) for g in groups]
        run = [run[g] + incl[g][:, 0:1] for g in groups]
        return i + 1, lowest(run), acc, run

    _, _, acc, _ = lax.while_loop(live, step, (jnp.int32(1), lowest(run), acc, run))
    for g in groups:
        o_ref[:, lanes[g]] = jnp.where(low, acc[g][:SB_Q], acc[g][SB_Q:]).astype(BF16)


def _stick_breaking(proj, batch, seq):
    assert SB_Q == SB_K and seq % SB_Q == 0
    t = proj.shape[0]
    nq = seq // SB_Q
    width = SB_GROUPS * LANES
    steps = SEG // width
    return pl.pallas_call(
        _sb_kernel,
        grid=(batch, steps, nq),
        in_specs=[
            pl.BlockSpec((SB_Q, width), lambda b, p, i: (b * nq + i, (SEG_SQ - PROJ_SEG0) * steps + p)),
            pl.BlockSpec((seq, width), lambda b, p, i: (b, (SEG_SK - PROJ_SEG0) * steps + p)),
            pl.BlockSpec((seq, width), lambda b, p, i: (b, (SEG_SV - PROJ_SEG0) * steps + p)),
        ],
        out_specs=pl.BlockSpec((SB_Q, width), lambda b, p, i: (b * nq + i, p)),
        out_shape=jax.ShapeDtypeStruct((t, SEG), BF16),
        compiler_params=_cparams(3),
        name="stick_breaking",
    )(proj, proj, proj)


FF_CHUNK = 1024


def _post_kernel(n_casts, x_ref, ret_ref, sb_ref, gu_ref, gv_ref, g0, g1, g2, g3, g4, g5,
                 sw_ref, sbias_ref, pr_ref, ps_ref, pg_ref, wo_ref, ln1g_ref, ln1b_ref,
                 wu_ref, wd_ref, ln2g_ref, ln2b_ref, *refs):
    o_ref = refs[n_casts]
    sg_ref = refs[-1]
    _run_casts(refs[:n_casts], refs[n_casts + 1:-1])
    row = lax.broadcasted_iota(jnp.int32, (CHUNK, CHUNK), 0)
    col = lax.broadcasted_iota(jnp.int32, (CHUNK, CHUNK), 1)
    causal = row >= col
    for g in range(SGU_GROUPS):
        wm = jnp.where(causal, sw_ref[g], 0.0).astype(BF16)
        bias = sbias_ref[g]
        gc = slice(g * LANES, (g + 1) * LANES)
        for c in range(x_ref.shape[0] // CHUNK):
            rc = slice(c * CHUNK, (c + 1) * CHUNK)
            sv = jnp.dot(wm, gv_ref[rc, gc], preferred_element_type=F32) + bias
            sg_ref[rc, gc] = (gu_ref[rc, gc].astype(F32) * sv).astype(BF16)

    gates = ((g0, g1), (g2, g3), (g4, g5))
    branches = ((ret_ref, pr_ref), (sb_ref, ps_ref), (sg_ref, pg_ref))
    halves = []
    for half in range(D_MODEL // SEG):
        cols = slice(half * SEG, (half + 1) * SEG)
        m = None
        for (a_ref, p_ref), gate in zip(branches, gates):
            term = gate[half][...].astype(F32) * jnp.dot(
                a_ref[...], p_ref[:, cols], preferred_element_type=F32)
            m = term if m is None else m + term
        halves.append(m.astype(BF16))
    y = (jnp.dot(halves[0], wo_ref[:SEG, :], preferred_element_type=F32)
         + jnp.dot(halves[1], wo_ref[SEG:, :], preferred_element_type=F32))
    x1 = _layer_norm(DEEPNORM_ALPHA * x_ref[...] + y, ln1g_ref[...], ln1b_ref[...])

    xb = x1.astype(BF16)
    acc = None
    for c in range(D_FF // FF_CHUNK):
        cols = slice(c * FF_CHUNK, (c + 1) * FF_CHUNK)
        h = jnp.maximum(jnp.dot(xb, wu_ref[:, cols], preferred_element_type=F32), 0.0)
        part = jnp.dot((h * h).astype(BF16), wd_ref[cols, :], preferred_element_type=F32)
        acc = part if acc is None else acc + part
    o_ref[...] = _layer_norm(DEEPNORM_ALPHA * x1 + acc, ln2g_ref[...], ln2b_ref[...])


def _post(x2, ret, sb, proj, sgu_w, b_tab, weights, ln1g, ln1b, ln2g, ln2b, to_cast, l, tm):
    t = x2.shape[0]
    p_ret, p_sb, p_sgu, w_out, w_up, w_down = weights
    row_spec = lambda w: pl.BlockSpec((tm, w), lambda i: (i, 0))
    seg_spec = lambda seg: pl.BlockSpec((tm, SEG), lambda i: (i, seg - PROJ_SEG0))
    cast_in, cast_out, cast_shapes = _cast_rider(to_cast, l + 1, t // tm)
    out = pl.pallas_call(
        functools.partial(_post_kernel, len(to_cast)),
        grid=(t // tm,),
        in_specs=[row_spec(D_MODEL), row_spec(SEG), row_spec(SEG),
                  seg_spec(SEG_GU), seg_spec(SEG_GV)]
        + [seg_spec(SEG_GATE0 + j) for j in range(6)]
        + [_of_layer((SGU_GROUPS, CHUNK, CHUNK), l), _of_layer((SGU_GROUPS, CHUNK, LANES), l)]
        + [_resident((SEG, D_MODEL))] * 3
        + [_resident((D_MODEL, D_MODEL)), _of_layer((1, D_MODEL), l), _of_layer((1, D_MODEL), l),
           _resident((D_MODEL, D_FF)), _resident((D_FF, D_MODEL)),
           _of_layer((1, D_MODEL), l), _of_layer((1, D_MODEL), l)]
        + cast_in,
        out_specs=[row_spec(D_MODEL)] + cast_out,
        out_shape=[jax.ShapeDtypeStruct((t, D_MODEL), F32)] + cast_shapes,
        scratch_shapes=[pltpu.VMEM((tm, SEG), BF16)],
        compiler_params=_cparams(1),
        name="post_mixer",
    )(x2, ret, sb, *([proj] * 8), sgu_w, b_tab, p_ret, p_sb, p_sgu, w_out, ln1g, ln1b,
      w_up, w_down, ln2g, ln2b, *to_cast)
    return out[0], out[1:]


def _rotary_tables(seq):
    half = RET_HEAD_DIM // 2
    inv_freq = ROPE_BASE ** (-np.arange(half, dtype=np.float64) / half)
    ang = np.arange(seq, dtype=np.float64)[:, None] * inv_freq[None, :]
    cos, sin = np.cos(ang), np.sin(ang)
    return (np.concatenate([cos, cos], axis=1).astype(np.float32),
            np.concatenate([-sin, sin], axis=1).astype(np.float32))


def _retention_tables():
    log_g = np.log(1.0 - 2.0 ** (-5.0 - np.arange(RET_HEADS, dtype=np.float64)))
    idx = np.arange(CHUNK, dtype=np.float64)
    diff = idx[:, None] - idx[None, :]
    dint = np.where(diff[None] >= 0, np.exp(log_g[:, None, None] * diff[None]), 0.0)
    k_decay = np.exp(log_g[:, None] * (CHUNK - 1 - idx)[None, :])
    q_decay = np.exp(log_g[:, None] * (idx + 1.0)[None, :])
    chunk_decay = np.exp(log_g * CHUNK)
    full = (RET_HEADS, CHUNK, RET_HEAD_DIM)
    return tuple(np.ascontiguousarray(a, dtype=np.float32) for a in (
        dint,
        np.broadcast_to(q_decay[:, :, None], full),
        np.broadcast_to(k_decay[:, :, None], full),
        np.broadcast_to(chunk_decay[:, None, None], full)))


def kernel(x, w_in, ret_gn_g, ret_gn_b, sgu_ln_g, sgu_ln_b, sgu_w, sgu_b, p_ret, p_sb, p_sgu,
           w_out, ln1_g, ln1_b, w_up, w_down, ln2_g, ln2_b):
    batch, seq, d = x.shape
    assert d == D_MODEL and seq % CHUNK == 0
    t = batch * seq
    tm = 512
    assert seq % tm == 0
    cs, sn = _rotary_tables(seq)
    ret_tabs = _retention_tables()
    rows = lambda a: a[:, None, :]
    b_tab = jnp.broadcast_to(sgu_b[:, :, :, None], sgu_b.shape + (LANES,))
    ret_gn_g, ret_gn_b, sgu_ln_g, sgu_ln_b, ln1_g, ln1_b, ln2_g, ln2_b = map(
        rows, (ret_gn_g, ret_gn_b, sgu_ln_g, sgu_ln_b, ln1_g, ln1_b, ln2_g, ln2_b))
    post_weights = (p_ret, p_sb, p_sgu, w_out, w_up, w_down)
    x2 = x.reshape(t, d)
    w_proj = w_in[0].astype(BF16)
    for l in range(DEPTH):
        proj, ret, weights = _in_proj(x2, w_proj, cs, sn, sgu_ln_g, sgu_ln_b, ret_tabs,
                                      ret_gn_g, ret_gn_b, post_weights, l, tm, seq)
        sb = _stick_breaking(proj, batch, seq)
        next_proj = (w_in,) if l + 1 < DEPTH else ()
        x2, cast = _post(x2, ret, sb, proj, sgu_w, b_tab, weights, ln1_g, ln1_b, ln2_g, ln2_b,
                         next_proj, l, tm)
        if cast:
            w_proj = cast[0]
    return x2.reshape(batch, seq, d)
```

```python
import functools

import jax
import jax.numpy as jnp
import numpy as np
from jax import lax
from jax.experimental import pallas as pl
from jax.experimental.pallas import tpu as pltpu

F32 = jnp.float32
BF16 = jnp.bfloat16

D_MODEL = 1024
DEPTH = 2
CHUNK = 128
RET_HEADS = 4
RET_HEAD_DIM = 128
SB_HEADS = 8
SB_HEAD_DIM = 64
SGU_GROUPS = 4
SEG = 512
D_FF = 4 * D_MODEL
N_IN = 9 * SEG + 3 * D_MODEL
ROPE_BASE = 10000.0
LN_EPS = 1e-5
DEEPNORM_ALPHA = (2 * DEPTH) ** 0.25

SEG_RQ, SEG_RK, SEG_RV, SEG_RG, SEG_SQ, SEG_SK, SEG_SV, SEG_GU, SEG_GV = range(9)
SEG_GATE0 = 9
N_SEG = N_IN // SEG
PROJ_SEG0 = SEG_GU
N_PROJ = (N_SEG - PROJ_SEG0) * SEG

LANES = 128
BF16_SUBLANES = 16
VMEM_LIMIT = 60 * 1024 * 1024


def _cparams(n_axes):
    return pltpu.CompilerParams(
        dimension_semantics=("arbitrary",) * n_axes,
        vmem_limit_bytes=VMEM_LIMIT)


def _resident(shape):
    zeros = (0,) * len(shape)
    return pl.BlockSpec(shape, lambda *_: zeros, pipeline_mode=pl.Buffered(1))


def _of_layer(shape, l):
    zeros = (0,) * len(shape)
    return pl.BlockSpec((None,) + tuple(shape), lambda *_: (l,) + zeros,
                        pipeline_mode=pl.Buffered(1))


def _cast_rider(params, l, n_steps):
    in_specs, out_specs, out_shapes = [], [], []
    for p in params:
        _, r, c = p.shape
        assert r % (n_steps * BF16_SUBLANES) == 0
        rb = r // n_steps
        in_specs.append(pl.BlockSpec((None, rb, c), lambda i: (l, i, 0)))
        out_specs.append(pl.BlockSpec((rb, c), lambda i: (i, 0)))
        out_shapes.append(jax.ShapeDtypeStruct((r, c), BF16))
    return in_specs, out_specs, out_shapes


def _run_casts(src_refs, dst_refs):
    for src, dst in zip(src_refs, dst_refs):
        dst[...] = src[...].astype(BF16)


def _layer_norm(x, g, b):
    mu = jnp.mean(x, axis=-1, keepdims=True)
    xc = x - mu
    var = jnp.mean(xc * xc, axis=-1, keepdims=True)
    return xc * lax.rsqrt(var + LN_EPS) * g + b


SB_Q = 256
SB_K = 256
SB_GROUPS = 2
LOG2E = 1.4426950408889634
SB_UNDERFLOW = 152.0
SB_MASKED = -1e30
SB_CLAMP = 126.0


class _SbUnit:
    def __init__(self, q_tile, k_ref, v_ref, col0, qi, tri, causal, low):
        self.groups = range(SB_GROUPS)
        self.k_ref, self.v_ref, self.qi, self.tri, self.causal, self.low = (
            k_ref, v_ref, qi, tri, causal, low)
        self.lanes = [slice(col0 + g * LANES, col0 + (g + 1) * LANES) for g in self.groups]
        self.qs = []
        for g in self.groups:
            q = q_tile[:, g * LANES:(g + 1) * LANES]
            zero = jnp.zeros_like(q)
            self.qs.append(jnp.concatenate(
                [jnp.where(low, q, zero), jnp.where(low, zero, q)], axis=0))

    @staticmethod
    def key_rows(kj):
        return pl.ds(pl.multiple_of(kj * SB_K, SB_K), SB_K)

    def scores(self, rows, g):
        return lax.dot_general(self.qs[g], self.k_ref[rows, self.lanes[g]],
                               (((1,), (1,)), ((), ())), preferred_element_type=F32)

    def suffix_sums(self, z):
        sp = jnp.maximum(jnp.log(1.0 + jnp.exp2(jnp.minimum(z, SB_CLAMP))) * LOG2E, z)
        return jnp.dot(sp.astype(BF16), self.tri, preferred_element_type=F32)

    def weighted(self, z, incl, run, rows, g):
        a = jnp.exp2(z - incl - run)
        return jnp.dot(a.astype(BF16), self.v_ref[rows, self.lanes[g]], preferred_element_type=F32)

    def stage_scores(self):
        self.rows_d = self.key_rows(self.qi)
        self.rows_p = self.key_rows(jnp.maximum(self.qi - 1, 0))
        self.z_d = [jnp.where(self.causal, self.scores(self.rows_d, g), SB_MASKED)
                    for g in self.groups]
        self.z_p = [self.scores(self.rows_p, g) for g in self.groups]

    def stage_sums(self):
        self.incl_d = [self.suffix_sums(z) for z in self.z_d]
        self.incl_p = [self.suffix_sums(z) for z in self.z_p]

    def stage_values(self):
        no_prev = jnp.where(self.qi == 0, -SB_MASKED, 0.0)
        run_p = [self.incl_d[g][:, 0:1] + no_prev for g in self.groups]
        self.acc = [self.weighted(self.z_d[g], self.incl_d[g], 0.0, self.rows_d, g)
                    + self.weighted(self.z_p[g], self.incl_p[g], run_p[g], self.rows_p, g)
                    for g in self.groups]
        self.run = [run_p[g] + self.incl_p[g][:, 0:1] for g in self.groups]

    def finish(self):
        qi = self.qi

        def lowest(run):
            return functools.reduce(jnp.minimum, [jnp.min(r) for r in run])

        def live(c):
            return jnp.logical_and(c[0] < qi, c[1] <= SB_UNDERFLOW)

        def step(c):
            i, _, acc, run = c
            rows = self.key_rows(qi - 1 - i)
            z = [self.scores(rows, g) for g in self.groups]
            incl = [self.suffix_sums(z[g]) for g in self.groups]
            acc = [acc[g] + self.weighted(z[g], incl[g], run[g], rows, g) for g in self.groups]
            run = [run[g] + incl[g][:, 0:1] for g in self.groups]
            return i + 1, lowest(run), acc, run

        _, _, acc, _ = lax.while_loop(live, step, (jnp.int32(1), lowest(self.run), self.acc, self.run))
        return jnp.concatenate(
            [jnp.where(self.low, acc[g][:SB_Q], acc[g][SB_Q:]).astype(BF16) for g in self.groups],
            axis=1)


def _inproj_kernel(n_casts, steps_per_seq, x_ref, w_ref, cs_ref, sn_ref, lng_ref, lnb_ref,
                   dint_ref, qd_ref, kd_ref, cd_ref, gng_ref, gnb_ref, *refs):
    o_ref, ret_ref, sb_ref = refs[n_casts:n_casts + 3]
    state_ref, ksc_ref, vsc_ref = refs[-3:]
    _run_casts(refs[:n_casts], refs[n_casts + 3:-3])
    step_in_seq = pl.program_id(0) % steps_per_seq

    @pl.when(step_in_seq == 0)
    def _():
        state_ref[...] = jnp.zeros_like(state_ref)

    tm = x_ref.shape[0]
    xb = x_ref[...].astype(BF16)
    cs = cs_ref[...]
    sn = sn_ref[...]
    heads = range(RET_HEADS)
    n_chunks = tm // CHUNK
    chains = [(c, h) for c in range(n_chunks) for h in heads]
    hcols = [slice(h * RET_HEAD_DIM, (h + 1) * RET_HEAD_DIM) for h in heads]
    crows = [slice(c * CHUNK, (c + 1) * CHUNK) for c in range(n_chunks)]
    nt = (((1,), (1,)), ((), ()))
    tn = (((0,), (0,)), ((), ()))

    def project(j):
        return jnp.dot(xb, w_ref[:, j * SEG:(j + 1) * SEG], preferred_element_type=F32)

    def rotary(acc, scale):
        out = []
        for h in heads:
            xh = acc[:, hcols[h]]
            r = xh * cs + pltpu.roll(xh, RET_HEAD_DIM // 2, axis=1) * sn
            out.append((r * scale).astype(BF16) if scale is not None else r.astype(BF16))
        return out

    def emit(j):
        acc = project(j)
        if j == SEG_GU:
            res = jax.nn.gelu(acc)
        elif j == SEG_GV:
            res = _layer_norm(jax.nn.gelu(acc), lng_ref[...], lnb_ref[...])
        else:
            res = jax.nn.sigmoid(acc)
        o_ref[:, (j - PROJ_SEG0) * SEG:(j - PROJ_SEG0 + 1) * SEG] = res.astype(BF16)

    tile_rows = pl.ds(pl.multiple_of(step_in_seq * tm, tm), tm)
    sq = (project(SEG_SQ) * (SB_HEAD_DIM ** -0.5 * LOG2E)).astype(BF16)
    ksc_ref[tile_rows, :] = project(SEG_SK).astype(BF16)
    vsc_ref[tile_rows, :] = project(SEG_SV).astype(BF16)

    low = lax.broadcasted_iota(jnp.int32, (SB_Q, LANES), 1) < SB_HEAD_DIM
    tri = jnp.where(lax.broadcasted_iota(jnp.int32, (SB_K, SB_K), 0)
                    >= lax.broadcasted_iota(jnp.int32, (SB_K, SB_K), 1), 1.0, 0.0).astype(BF16)
    causal = (lax.broadcasted_iota(jnp.int32, (2 * SB_Q, SB_K), 1)
              < (lax.broadcasted_iota(jnp.int32, (2 * SB_Q, SB_K), 0) & (SB_Q - 1)))
    unit_cols = SB_GROUPS * LANES
    units = []
    for qb in range(tm // SB_Q):
        for u in range(SEG // unit_cols):
            q_tile = sq[qb * SB_Q:(qb + 1) * SB_Q, u * unit_cols:(u + 1) * unit_cols]
            units.append((qb, u, _SbUnit(q_tile, ksc_ref, vsc_ref, u * unit_cols,
                                         step_in_seq * (tm // SB_Q) + qb, tri, causal, low)))
    sb_stages = [stage for _, _, unit in units
                 for stage in (unit.stage_scores, unit.stage_sums, unit.stage_values)]

    def sb_advance(n):
        for _ in range(n):
            if sb_stages:
                sb_stages.pop(0)()

    sb_advance(1)
    rq = rotary(project(SEG_RQ), None)
    sb_advance(1)
    rk = rotary(project(SEG_RK), RET_HEAD_DIM ** -0.5)
    sb_advance(1)
    rv = project(SEG_RV).astype(BF16)
    sb_advance(1)
    rg = jax.nn.silu(project(SEG_RG)).astype(BF16)
    q = {(c, h): rq[h][crows[c]] for c, h in chains}
    k = {(c, h): rk[h][crows[c]] for c, h in chains}
    v = {(c, h): rv[crows[c], hcols[h]] for c, h in chains}
    s = {(c, h): lax.dot_general(q[c, h], k[c, h], nt, preferred_element_type=F32) * dint_ref[h]
         for c, h in chains}
    kdec = {(c, h): (k[c, h].astype(F32) * kd_ref[h]).astype(BF16) for c, h in chains}
    kv = {(c, h): lax.dot_general(kdec[c, h], v[c, h], tn, preferred_element_type=F32)
          for c, h in chains}
    qdec = {(c, h): (q[c, h].astype(F32) * qd_ref[h]).astype(BF16) for c, h in chains}
    sb_advance(1)
    emit(SEG_GU)
    sb_advance(1)
    st = {(0, h): state_ref[h] for h in heads}
    for c in range(n_chunks):
        for h in heads:
            st[c + 1, h] = st[c, h] * cd_ref[h] + kv[c, h]
    for h in heads:
        state_ref[h] = st[n_chunks, h]
    ret = {(c, h): jnp.dot(jnp.concatenate([s[c, h].astype(BF16), qdec[c, h]], axis=1),
                           jnp.concatenate([v[c, h], st[c, h].astype(BF16)], axis=0),
                           preferred_element_type=F32)
           for c, h in chains}
    emit(SEG_GV)
    sb_advance(1)
    for c, h in chains:
        y = _layer_norm(ret[c, h], gng_ref[:, hcols[h]], gnb_ref[:, hcols[h]])
        ret_ref[crows[c], hcols[h]] = (rg[crows[c], hcols[h]].astype(F32) * y).astype(BF16)
    for j in range(SEG_GATE0, N_SEG):
        emit(j)
        sb_advance(1)
    sb_advance(len(sb_stages))
    for qb, u, unit in units:
        sb_ref[qb * SB_Q:(qb + 1) * SB_Q, u * unit_cols:(u + 1) * unit_cols] = unit.finish()


def _in_proj(x2, w, cs, sn, lng, lnb, ret_tabs, gng, gnb, to_cast, l, tm, seq):
    t = x2.shape[0]
    pos_blocks = seq // tm
    cast_in, cast_out, cast_shapes = _cast_rider(to_cast, l, t // tm)
    tab_spec = _resident((RET_HEADS, CHUNK, RET_HEAD_DIM))
    out = pl.pallas_call(
        functools.partial(_inproj_kernel, len(to_cast), pos_blocks),
        grid=(t // tm,),
        in_specs=[
            pl.BlockSpec((tm, D_MODEL), lambda i: (i, 0)),
            _resident((D_MODEL, N_IN)),
            pl.BlockSpec((tm, LANES), lambda i: (i % pos_blocks, 0)),
            pl.BlockSpec((tm, LANES), lambda i: (i % pos_blocks, 0)),
            _of_layer((1, SEG), l),
            _of_layer((1, SEG), l),
            tab_spec, tab_spec, tab_spec, tab_spec,
            _of_layer((1, SEG), l),
            _of_layer((1, SEG), l),
        ] + cast_in,
        out_specs=[pl.BlockSpec((tm, N_PROJ), lambda i: (i, 0)),
                   pl.BlockSpec((tm, SEG), lambda i: (i, 0)),
                   pl.BlockSpec((tm, SEG), lambda i: (i, 0))] + cast_out,
        out_shape=[jax.ShapeDtypeStruct((t, N_PROJ), BF16),
                   jax.ShapeDtypeStruct((t, SEG), BF16),
                   jax.ShapeDtypeStruct((t, SEG), BF16)] + cast_shapes,
        scratch_shapes=[pltpu.VMEM((RET_HEADS, RET_HEAD_DIM, RET_HEAD_DIM), F32),
                        pltpu.VMEM((seq, SEG), BF16), pltpu.VMEM((seq, SEG), BF16)],
        compiler_params=_cparams(1),
        name="in_proj",
    )(x2, w, cs, sn, lng, lnb, *ret_tabs, gng, gnb, *to_cast)
    return out[0], out[1], out[2], out[3:]


FF_CHUNK = 1024


def _post_kernel(n_casts, x_ref, ret_ref, sb_ref, gu_ref, gv_ref, g0, g1, g2, g3, g4, g5,
                 sw_ref, sbias_ref, pr_ref, ps_ref, pg_ref, wo_ref, ln1g_ref, ln1b_ref,
                 wu_ref, wd_ref, ln2g_ref, ln2b_ref, *refs):
    o_ref = refs[n_casts]
    sg_ref = refs[-1]
    _run_casts(refs[:n_casts], refs[n_casts + 1:-1])
    row = lax.broadcasted_iota(jnp.int32, (CHUNK, CHUNK), 0)
    col = lax.broadcasted_iota(jnp.int32, (CHUNK, CHUNK), 1)
    causal = row >= col
    for g in range(SGU_GROUPS):
        wm = jnp.where(causal, sw_ref[g], 0.0).astype(BF16)
        bias = sbias_ref[g]
        gc = slice(g * LANES, (g + 1) * LANES)
        for c in range(x_ref.shape[0] // CHUNK):
            rc = slice(c * CHUNK, (c + 1) * CHUNK)
            sv = jnp.dot(wm, gv_ref[rc, gc], preferred_element_type=F32) + bias
            sg_ref[rc, gc] = (gu_ref[rc, gc].astype(F32) * sv).astype(BF16)

    gates = ((g0, g1), (g2, g3), (g4, g5))
    branches = ((ret_ref, pr_ref), (sb_ref, ps_ref), (sg_ref, pg_ref))
    halves = []
    for half in range(D_MODEL // SEG):
        cols = slice(half * SEG, (half + 1) * SEG)
        m = None
        for (a_ref, p_ref), gate in zip(branches, gates):
            term = gate[half][...].astype(F32) * jnp.dot(
                a_ref[...], p_ref[:, cols], preferred_element_type=F32)
            m = term if m is None else m + term
        halves.append(m.astype(BF16))
    y = (jnp.dot(halves[0], wo_ref[:SEG, :], preferred_element_type=F32)
         + jnp.dot(halves[1], wo_ref[SEG:, :], preferred_element_type=F32))
    x1 = _layer_norm(DEEPNORM_ALPHA * x_ref[...] + y, ln1g_ref[...], ln1b_ref[...])

    xb = x1.astype(BF16)
    acc = None
    for c in range(D_FF // FF_CHUNK):
        cols = slice(c * FF_CHUNK, (c + 1) * FF_CHUNK)
        h = jnp.maximum(jnp.dot(xb, wu_ref[:, cols], preferred_element_type=F32), 0.0)
        part = jnp.dot((h * h).astype(BF16), wd_ref[cols, :], preferred_element_type=F32)
        acc = part if acc is None else acc + part
    o_ref[...] = _layer_norm(DEEPNORM_ALPHA * x1 + acc, ln2g_ref[...], ln2b_ref[...])


def _post(x2, ret, sb, proj, sgu_w, b_tab, weights, ln1g, ln1b, ln2g, ln2b, to_cast, l, tm):
    t = x2.shape[0]
    p_ret, p_sb, p_sgu, w_out, w_up, w_down = weights
    row_spec = lambda w: pl.BlockSpec((tm, w), lambda i: (i, 0))
    seg_spec = lambda seg: pl.BlockSpec((tm, SEG), lambda i: (i, seg - PROJ_SEG0))
    cast_in, cast_out, cast_shapes = _cast_rider(to_cast, l + 1, t // tm)
    out = pl.pallas_call(
        functools.partial(_post_kernel, len(to_cast)),
        grid=(t // tm,),
        in_specs=[row_spec(D_MODEL), row_spec(SEG), row_spec(SEG),
                  seg_spec(SEG_GU), seg_spec(SEG_GV)]
        + [seg_spec(SEG_GATE0 + j) for j in range(6)]
        + [_of_layer((SGU_GROUPS, CHUNK, CHUNK), l), _of_layer((SGU_GROUPS, CHUNK, LANES), l)]
        + [_resident((SEG, D_MODEL))] * 3
        + [_resident((D_MODEL, D_MODEL)), _of_layer((1, D_MODEL), l), _of_layer((1, D_MODEL), l),
           _resident((D_MODEL, D_FF)), _resident((D_FF, D_MODEL)),
           _of_layer((1, D_MODEL), l), _of_layer((1, D_MODEL), l)]
        + cast_in,
        out_specs=[row_spec(D_MODEL)] + cast_out,
        out_shape=[jax.ShapeDtypeStruct((t, D_MODEL), F32)] + cast_shapes,
        scratch_shapes=[pltpu.VMEM((tm, SEG), BF16)],
        compiler_params=_cparams(1),
        name="post_mixer",
    )(x2, ret, sb, *([proj] * 8), sgu_w, b_tab, p_ret, p_sb, p_sgu, w_out, ln1g, ln1b,
      w_up, w_down, ln2g, ln2b, *to_cast)
    return out[0], out[1:]


def _rotary_tables(seq):
    half = RET_HEAD_DIM // 2
    inv_freq = ROPE_BASE ** (-np.arange(half, dtype=np.float64) / half)
    ang = np.arange(seq, dtype=np.float64)[:, None] * inv_freq[None, :]
    cos, sin = np.cos(ang), np.sin(ang)
    return (np.concatenate([cos, cos], axis=1).astype(np.float32),
            np.concatenate([-sin, sin], axis=1).astype(np.float32))


def _retention_tables():
    log_g = np.log(1.0 - 2.0 ** (-5.0 - np.arange(RET_HEADS, dtype=np.float64)))
    idx = np.arange(CHUNK, dtype=np.float64)
    diff = idx[:, None] - idx[None, :]
    dint = np.where(diff[None] >= 0, np.exp(log_g[:, None, None] * diff[None]), 0.0)
    k_decay = np.exp(log_g[:, None] * (CHUNK - 1 - idx)[None, :])
    q_decay = np.exp(log_g[:, None] * (idx + 1.0)[None, :])
    chunk_decay = np.exp(log_g * CHUNK)
    full = (RET_HEADS, CHUNK, RET_HEAD_DIM)
    return tuple(np.ascontiguousarray(a, dtype=np.float32) for a in (
        dint,
        np.broadcast_to(q_decay[:, :, None], full),
        np.broadcast_to(k_decay[:, :, None], full),
        np.broadcast_to(chunk_decay[:, None, None], full)))


def kernel(x, w_in, ret_gn_g, ret_gn_b, sgu_ln_g, sgu_ln_b, sgu_w, sgu_b, p_ret, p_sb, p_sgu,
           w_out, ln1_g, ln1_b, w_up, w_down, ln2_g, ln2_b):
    batch, seq, d = x.shape
    assert d == D_MODEL and seq % CHUNK == 0
    t = batch * seq
    tm = 512
    assert seq % tm == 0 and tm % SB_Q == 0 and SB_Q == SB_K
    cs, sn = _rotary_tables(seq)
    ret_tabs = _retention_tables()
    rows = lambda a: a[:, None, :]
    b_tab = jnp.broadcast_to(sgu_b[:, :, :, None], sgu_b.shape + (LANES,))
    ret_gn_g, ret_gn_b, sgu_ln_g, sgu_ln_b, ln1_g, ln1_b, ln2_g, ln2_b = map(
        rows, (ret_gn_g, ret_gn_b, sgu_ln_g, sgu_ln_b, ln1_g, ln1_b, ln2_g, ln2_b))
    post_weights = (p_ret, p_sb, p_sgu, w_out, w_up, w_down)
    x2 = x.reshape(t, d)
    w_proj = w_in[0].astype(BF16)
    for l in range(DEPTH):
        proj, ret, sb, weights = _in_proj(x2, w_proj, cs, sn, sgu_ln_g, sgu_ln_b, ret_tabs,
                                          ret_gn_g, ret_gn_b, post_weights, l, tm, seq)
        next_proj = (w_in,) if l + 1 < DEPTH else ()
        x2, cast = _post(x2, ret, sb, proj, sgu_w, b_tab, weights, ln1_g, ln1_b, ln2_g, ln2_b,
                         next_proj, l, tm)
        if cast:
            w_proj = cast[0]
    return x2.reshape(batch, seq, d)
```

```python
import functools

import jax
import jax.numpy as jnp
import numpy as np
from jax import lax
from jax.experimental import pallas as pl
from jax.experimental.pallas import tpu as pltpu

F32 = jnp.float32
BF16 = jnp.bfloat16

D_MODEL = 1024
DEPTH = 2
CHUNK = 128
RET_HEADS = 4
RET_HEAD_DIM = 128
SB_HEADS = 8
SB_HEAD_DIM = 64
SGU_GROUPS = 4
SEG = 512
D_FF = 4 * D_MODEL
N_IN = 9 * SEG + 3 * D_MODEL
ROPE_BASE = 10000.0
LN_EPS = 1e-5
DEEPNORM_ALPHA = (2 * DEPTH) ** 0.25

SEG_RQ, SEG_RK, SEG_RV, SEG_RG, SEG_SQ, SEG_SK, SEG_SV, SEG_GU, SEG_GV = range(9)
SEG_GATE0 = 9
N_SEG = N_IN // SEG
PROJ_SEG0 = SEG_GU
N_PROJ = (N_SEG - PROJ_SEG0) * SEG

LANES = 128
BF16_SUBLANES = 16
VMEM_LIMIT = 60 * 1024 * 1024


def _cparams(n_axes):
    return pltpu.CompilerParams(
        dimension_semantics=("arbitrary",) * n_axes,
        vmem_limit_bytes=VMEM_LIMIT)


def _resident(shape):
    zeros = (0,) * len(shape)
    return pl.BlockSpec(shape, lambda *_: zeros, pipeline_mode=pl.Buffered(1))


def _of_layer(shape, l):
    zeros = (0,) * len(shape)
    return pl.BlockSpec((None,) + tuple(shape), lambda *_: (l,) + zeros,
                        pipeline_mode=pl.Buffered(1))


def _cast_rider(params, l, n_steps):
    in_specs, out_specs, out_shapes = [], [], []
    for p in params:
        _, r, c = p.shape
        assert r % (n_steps * BF16_SUBLANES) == 0
        rb = r // n_steps
        in_specs.append(pl.BlockSpec((None, rb, c), lambda i: (l, i, 0)))
        out_specs.append(pl.BlockSpec((rb, c), lambda i: (i, 0)))
        out_shapes.append(jax.ShapeDtypeStruct((r, c), BF16))
    return in_specs, out_specs, out_shapes


def _run_casts(src_refs, dst_refs):
    for src, dst in zip(src_refs, dst_refs):
        dst[...] = src[...].astype(BF16)


def _layer_norm(x, g, b):
    mu = jnp.mean(x, axis=-1, keepdims=True)
    xc = x - mu
    var = jnp.mean(xc * xc, axis=-1, keepdims=True)
    return xc * lax.rsqrt(var + LN_EPS) * g + b


SB_Q = 256
SB_K = 256
SB_GROUPS = 2
LOG2E = 1.4426950408889634
SB_UNDERFLOW = 152.0
SB_MASKED = -1e30
SB_CLAMP = 126.0


class _SbUnit:
    def __init__(self, q_tile, k_ref, v_ref, acc_ref, run_ref, col0, qi, tri, causal, low):
        self.groups = range(SB_GROUPS)
        self.k_ref, self.v_ref, self.qi, self.tri, self.causal, self.low = (
            k_ref, v_ref, qi, tri, causal, low)
        self.acc_ref, self.run_ref = acc_ref, run_ref
        self.lanes = [slice(col0 + g * LANES, col0 + (g + 1) * LANES) for g in self.groups]
        self.qs = []
        for g in self.groups:
            q = q_tile[:, g * LANES:(g + 1) * LANES]
            zero = jnp.zeros_like(q)
            self.qs.append(jnp.concatenate(
                [jnp.where(low, q, zero), jnp.where(low, zero, q)], axis=0))

    @staticmethod
    def key_rows(kj):
        return pl.ds(pl.multiple_of(kj * SB_K, SB_K), SB_K)

    def scores(self, rows, g):
        return lax.dot_general(self.qs[g], self.k_ref[rows, self.lanes[g]],
                               (((1,), (1,)), ((), ())), preferred_element_type=F32)

    def suffix_sums(self, z):
        sp = jnp.maximum(jnp.log(1.0 + jnp.exp2(jnp.minimum(z, SB_CLAMP))) * LOG2E, z)
        return jnp.dot(sp.astype(BF16), self.tri, preferred_element_type=F32)

    def weighted(self, z, incl, run, rows, g):
        a = jnp.exp2(z - incl - run)
        return jnp.dot(a.astype(BF16), self.v_ref[rows, self.lanes[g]], preferred_element_type=F32)

    def stage_scores(self):
        self.rows_d = self.key_rows(self.qi)
        self.rows_p = self.key_rows(jnp.maximum(self.qi - 1, 0))
        self.z_d = [jnp.where(self.causal, self.scores(self.rows_d, g), SB_MASKED)
                    for g in self.groups]
        self.z_p = [self.scores(self.rows_p, g) for g in self.groups]

    def stage_sums(self):
        self.incl_d = [self.suffix_sums(z) for z in self.z_d]
        self.incl_p = [self.suffix_sums(z) for z in self.z_p]

    def stage_values(self):
        no_prev = jnp.where(self.qi == 0, -SB_MASKED, 0.0)
        run_p = [self.incl_d[g][:, 0:1] + no_prev for g in self.groups]
        run = [run_p[g] + self.incl_p[g][:, 0:1] for g in self.groups]
        for g in self.groups:
            self.acc_ref[g] = (self.weighted(self.z_d[g], self.incl_d[g], 0.0, self.rows_d, g)
                               + self.weighted(self.z_p[g], self.incl_p[g], run_p[g], self.rows_p, g))
            self.run_ref[g] = run[g]
        self.lowest_run = self.lowest(run)

    @staticmethod
    def lowest(run):
        return functools.reduce(jnp.minimum, [jnp.min(r) for r in run])

    def finish(self):
        qi = self.qi

        def live(c):
            return jnp.logical_and(c[0] < qi, c[1] <= SB_UNDERFLOW)

        def step(c):
            i = c[0]
            rows = self.key_rows(qi - 1 - i)
            z = [self.scores(rows, g) for g in self.groups]
            incl = [self.suffix_sums(z[g]) for g in self.groups]
            run = [self.run_ref[g] for g in self.groups]
            for g in self.groups:
                self.acc_ref[g] += self.weighted(z[g], incl[g], run[g], rows, g)
                run[g] = run[g] + incl[g][:, 0:1]
                self.run_ref[g] = run[g]
            return i + 1, self.lowest(run)

        @pl.when(live((jnp.int32(1), self.lowest_run)))
        def _():
            lax.while_loop(live, step, (jnp.int32(1), self.lowest_run))

        return jnp.concatenate(
            [jnp.where(self.low, self.acc_ref[g, :SB_Q], self.acc_ref[g, SB_Q:]).astype(BF16)
             for g in self.groups], axis=1)


def _inproj_kernel(n_casts, steps_per_seq, l, x_ref, w_ref, cs_ref, sn_ref, lng_ref, lnb_ref,
                   dint_ref, qd_ref, kd_ref, cd_ref, gng_ref, gnb_ref, *refs):
    o_ref, ret_ref, sb_ref = refs[n_casts:n_casts + 3]
    state_ref, ksc_ref, vsc_ref, acc_ref, run_ref = refs[-5:]
    _run_casts(refs[:n_casts], refs[n_casts + 3:-5])
    step_in_seq = pl.program_id(0) % steps_per_seq

    @pl.when(step_in_seq == 0)
    def _():
        state_ref[...] = jnp.zeros_like(state_ref)

    tm = x_ref.shape[0]
    xb = x_ref[...].astype(BF16)
    cs = cs_ref[...]
    sn = sn_ref[...]
    heads = range(RET_HEADS)
    n_chunks = tm // CHUNK
    chains = [(c, h) for c in range(n_chunks) for h in heads]
    hcols = [slice(h * RET_HEAD_DIM, (h + 1) * RET_HEAD_DIM) for h in heads]
    crows = [slice(c * CHUNK, (c + 1) * CHUNK) for c in range(n_chunks)]
    nt = (((1,), (1,)), ((), ()))
    tn = (((0,), (0,)), ((), ()))

    def project(j):
        return jnp.dot(xb, w_ref[:, j * SEG:(j + 1) * SEG], preferred_element_type=F32)

    def rotary(acc, scale):
        out = []
        for h in heads:
            xh = acc[:, hcols[h]]
            r = xh * cs + pltpu.roll(xh, RET_HEAD_DIM // 2, axis=1) * sn
            out.append((r * scale).astype(BF16) if scale is not None else r.astype(BF16))
        return out

    def emit(j):
        acc = project(j)
        if j == SEG_GU:
            res = jax.nn.gelu(acc)
        elif j == SEG_GV:
            res = _layer_norm(jax.nn.gelu(acc), lng_ref[l:l + 1, :], lnb_ref[l:l + 1, :])
        else:
            res = jax.nn.sigmoid(acc)
        o_ref[:, (j - PROJ_SEG0) * SEG:(j - PROJ_SEG0 + 1) * SEG] = res.astype(BF16)

    tile_rows = pl.ds(pl.multiple_of(step_in_seq * tm, tm), tm)
    sq = (project(SEG_SQ) * (SB_HEAD_DIM ** -0.5 * LOG2E)).astype(BF16)
    ksc_ref[tile_rows, :] = project(SEG_SK).astype(BF16)
    vsc_ref[tile_rows, :] = project(SEG_SV).astype(BF16)

    low = lax.broadcasted_iota(jnp.int32, (SB_Q, LANES), 1) < SB_HEAD_DIM
    tri = jnp.where(lax.broadcasted_iota(jnp.int32, (SB_K, SB_K), 0)
                    >= lax.broadcasted_iota(jnp.int32, (SB_K, SB_K), 1), 1.0, 0.0).astype(BF16)
    causal = (lax.broadcasted_iota(jnp.int32, (2 * SB_Q, SB_K), 1)
              < (lax.broadcasted_iota(jnp.int32, (2 * SB_Q, SB_K), 0) & (SB_Q - 1)))
    unit_cols = SB_GROUPS * LANES
    units = []
    for qb in range(tm // SB_Q):
        for u in range(SEG // unit_cols):
            q_tile = sq[qb * SB_Q:(qb + 1) * SB_Q, u * unit_cols:(u + 1) * unit_cols]
            units.append((qb, u, _SbUnit(q_tile, ksc_ref, vsc_ref, acc_ref.at[len(units)],
                                         run_ref.at[len(units)], u * unit_cols,
                                         step_in_seq * (tm // SB_Q) + qb, tri, causal, low)))
    sb_stages = [stage for _, _, unit in units
                 for stage in (unit.stage_scores, unit.stage_sums, unit.stage_values)]

    def sb_advance(n):
        for _ in range(n):
            if sb_stages:
                sb_stages.pop(0)()

    sb_advance(1)
    rq = rotary(project(SEG_RQ), None)
    sb_advance(1)
    rk = rotary(project(SEG_RK), RET_HEAD_DIM ** -0.5)
    sb_advance(1)
    rv = project(SEG_RV).astype(BF16)
    sb_advance(1)
    rg = jax.nn.silu(project(SEG_RG)).astype(BF16)
    q = {(c, h): rq[h][crows[c]] for c, h in chains}
    k = {(c, h): rk[h][crows[c]] for c, h in chains}
    v = {(c, h): rv[crows[c], hcols[h]] for c, h in chains}
    s = {(c, h): lax.dot_general(q[c, h], k[c, h], nt, preferred_element_type=F32) * dint_ref[h]
         for c, h in chains}
    kdec = {(c, h): (k[c, h].astype(F32) * kd_ref[h]).astype(BF16) for c, h in chains}
    kv = {(c, h): lax.dot_general(kdec[c, h], v[c, h], tn, preferred_element_type=F32)
          for c, h in chains}
    qdec = {(c, h): (q[c, h].astype(F32) * qd_ref[h]).astype(BF16) for c, h in chains}
    sb_advance(1)
    emit(SEG_GU)
    sb_advance(1)
    st = {(0, h): state_ref[h] for h in heads}
    for c in range(n_chunks):
        for h in heads:
            st[c + 1, h] = st[c, h] * cd_ref[h] + kv[c, h]
    for h in heads:
        state_ref[h] = st[n_chunks, h]
    ret = {(c, h): jnp.dot(jnp.concatenate([s[c, h].astype(BF16), qdec[c, h]], axis=1),
                           jnp.concatenate([v[c, h], st[c, h].astype(BF16)], axis=0),
                           preferred_element_type=F32)
           for c, h in chains}
    emit(SEG_GV)
    sb_advance(1)
    for c, h in chains:
        y = _layer_norm(ret[c, h], gng_ref[l:l + 1, hcols[h]], gnb_ref[l:l + 1, hcols[h]])
        ret_ref[crows[c], hcols[h]] = (rg[crows[c], hcols[h]].astype(F32) * y).astype(BF16)
    for j in range(SEG_GATE0, N_SEG):
        emit(j)
        sb_advance(1)
    sb_advance(len(sb_stages))
    for qb, u, unit in units:
        sb_ref[qb * SB_Q:(qb + 1) * SB_Q, u * unit_cols:(u + 1) * unit_cols] = unit.finish()


def _in_proj(x2, w, cs, sn, lng, lnb, ret_tabs, gng, gnb, to_cast, l, tm, seq):
    t = x2.shape[0]
    pos_blocks = seq // tm
    cast_in, cast_out, cast_shapes = _cast_rider(to_cast, l, t // tm)
    tab_spec = _resident((RET_HEADS, CHUNK, RET_HEAD_DIM))
    n_units = (tm // SB_Q) * (SEG // (SB_GROUPS * LANES))
    out = pl.pallas_call(
        functools.partial(_inproj_kernel, len(to_cast), pos_blocks, l),
        grid=(t // tm,),
        in_specs=[
            pl.BlockSpec((tm, D_MODEL), lambda i: (i, 0)),
            _resident((D_MODEL, N_IN)),
            pl.BlockSpec((tm, LANES), lambda i: (i % pos_blocks, 0)),
            pl.BlockSpec((tm, LANES), lambda i: (i % pos_blocks, 0)),
            _resident(lng.shape),
            _resident(lnb.shape),
            tab_spec, tab_spec, tab_spec, tab_spec,
            _resident(gng.shape),
            _resident(gnb.shape),
        ] + cast_in,
        out_specs=[pl.BlockSpec((tm, N_PROJ), lambda i: (i, 0)),
                   pl.BlockSpec((tm, SEG), lambda i: (i, 0)),
                   pl.BlockSpec((tm, SEG), lambda i: (i, 0))] + cast_out,
        out_shape=[jax.ShapeDtypeStruct((t, N_PROJ), BF16),
                   jax.ShapeDtypeStruct((t, SEG), BF16),
                   jax.ShapeDtypeStruct((t, SEG), BF16)] + cast_shapes,
        scratch_shapes=[pltpu.VMEM((RET_HEADS, RET_HEAD_DIM, RET_HEAD_DIM), F32),
                        pltpu.VMEM((seq, SEG), BF16), pltpu.VMEM((seq, SEG), BF16),
                        pltpu.VMEM((n_units, SB_GROUPS, 2 * SB_Q, LANES), F32),
                        pltpu.VMEM((n_units, SB_GROUPS, 2 * SB_Q, 1), F32)],
        compiler_params=_cparams(1),
        name="in_proj",
    )(x2, w, cs, sn, lng, lnb, *ret_tabs, gng, gnb, *to_cast)
    return out[0], out[1], out[2], out[3:]


FF_CHUNK = 1024


def _post_kernel(n_casts, l, x_ref, ret_ref, sb_ref, gu_ref, gv_ref, g0, g1, g2, g3, g4, g5,
                 sw_ref, sbias_ref, pr_ref, ps_ref, pg_ref, wo_ref, ln1g_ref, ln1b_ref,
                 wu_ref, wd_ref, ln2g_ref, ln2b_ref, *refs):
    o_ref = refs[n_casts]
    sg_ref = refs[-1]
    _run_casts(refs[:n_casts], refs[n_casts + 1:-1])
    row = lax.broadcasted_iota(jnp.int32, (CHUNK, CHUNK), 0)
    col = lax.broadcasted_iota(jnp.int32, (CHUNK, CHUNK), 1)
    causal = row >= col
    for g in range(SGU_GROUPS):
        wm = jnp.where(causal, sw_ref[g], 0.0).astype(BF16)
        bias = sbias_ref[g]
        gc = slice(g * LANES, (g + 1) * LANES)
        for c in range(x_ref.shape[0] // CHUNK):
            rc = slice(c * CHUNK, (c + 1) * CHUNK)
            sv = jnp.dot(wm, gv_ref[rc, gc], preferred_element_type=F32) + bias
            sg_ref[rc, gc] = (gu_ref[rc, gc].astype(F32) * sv).astype(BF16)

    gates = ((g0, g1), (g2, g3), (g4, g5))
    branches = ((ret_ref, pr_ref), (sb_ref, ps_ref), (sg_ref, pg_ref))
    halves = []
    for half in range(D_MODEL // SEG):
        cols = slice(half * SEG, (half + 1) * SEG)
        m = None
        for (a_ref, p_ref), gate in zip(branches, gates):
            term = gate[half][...].astype(F32) * jnp.dot(
                a_ref[...], p_ref[:, cols], preferred_element_type=F32)
            m = term if m is None else m + term
        halves.append(m.astype(BF16))
    y = (jnp.dot(halves[0], wo_ref[:SEG, :], preferred_element_type=F32)
         + jnp.dot(halves[1], wo_ref[SEG:, :], preferred_element_type=F32))
    x1 = _layer_norm(DEEPNORM_ALPHA * x_ref[...] + y, ln1g_ref[l:l + 1, :], ln1b_ref[l:l + 1, :])

    xb = x1.astype(BF16)
    acc = None
    for c in range(D_FF // FF_CHUNK):
        cols = slice(c * FF_CHUNK, (c + 1) * FF_CHUNK)
        h = jnp.maximum(jnp.dot(xb, wu_ref[:, cols], preferred_element_type=F32), 0.0)
        part = jnp.dot((h * h).astype(BF16), wd_ref[cols, :], preferred_element_type=F32)
        acc = part if acc is None else acc + part
    o_ref[...] = _layer_norm(DEEPNORM_ALPHA * x1 + acc, ln2g_ref[l:l + 1, :], ln2b_ref[l:l + 1, :])


def _post(x2, ret, sb, proj, sgu_w, b_tab, weights, ln1g, ln1b, ln2g, ln2b, to_cast, l, tm):
    t = x2.shape[0]
    p_ret, p_sb, p_sgu, w_out, w_up, w_down = weights
    row_spec = lambda w: pl.BlockSpec((tm, w), lambda i: (i, 0))
    seg_spec = lambda seg: pl.BlockSpec((tm, SEG), lambda i: (i, seg - PROJ_SEG0))
    cast_in, cast_out, cast_shapes = _cast_rider(to_cast, l + 1, t // tm)
    out = pl.pallas_call(
        functools.partial(_post_kernel, len(to_cast), l),
        grid=(t // tm,),
        in_specs=[row_spec(D_MODEL), row_spec(SEG), row_spec(SEG),
                  seg_spec(SEG_GU), seg_spec(SEG_GV)]
        + [seg_spec(SEG_GATE0 + j) for j in range(6)]
        + [_of_layer((SGU_GROUPS, CHUNK, CHUNK), l), _of_layer((SGU_GROUPS, CHUNK, LANES), l)]
        + [_resident((SEG, D_MODEL))] * 3
        + [_resident((D_MODEL, D_MODEL)), _resident(ln1g.shape), _resident(ln1b.shape),
           _resident((D_MODEL, D_FF)), _resident((D_FF, D_MODEL)),
           _resident(ln2g.shape), _resident(ln2b.shape)]
        + cast_in,
        out_specs=[row_spec(D_MODEL)] + cast_out,
        out_shape=[jax.ShapeDtypeStruct((t, D_MODEL), F32)] + cast_shapes,
        scratch_shapes=[pltpu.VMEM((tm, SEG), BF16)],
        compiler_params=_cparams(1),
        name="post_mixer",
    )(x2, ret, sb, *([proj] * 8), sgu_w, b_tab, p_ret, p_sb, p_sgu, w_out, ln1g, ln1b,
      w_up, w_down, ln2g, ln2b, *to_cast)
    return out[0], out[1:]


def _rotary_tables(seq):
    half = RET_HEAD_DIM // 2
    inv_freq = ROPE_BASE ** (-np.arange(half, dtype=np.float64) / half)
    ang = np.arange(seq, dtype=np.float64)[:, None] * inv_freq[None, :]
    cos, sin = np.cos(ang), np.sin(ang)
    return (np.concatenate([cos, cos], axis=1).astype(np.float32),
            np.concatenate([-sin, sin], axis=1).astype(np.float32))


def _retention_tables():
    log_g = np.log(1.0 - 2.0 ** (-5.0 - np.arange(RET_HEADS, dtype=np.float64)))
    idx = np.arange(CHUNK, dtype=np.float64)
    diff = idx[:, None] - idx[None, :]
    dint = np.where(diff[None] >= 0, np.exp(log_g[:, None, None] * diff[None]), 0.0)
    k_decay = np.exp(log_g[:, None] * (CHUNK - 1 - idx)[None, :])
    q_decay = np.exp(log_g[:, None] * (idx + 1.0)[None, :])
    chunk_decay = np.exp(log_g * CHUNK)
    full = (RET_HEADS, CHUNK, RET_HEAD_DIM)
    return tuple(np.ascontiguousarray(a, dtype=np.float32) for a in (
        dint,
        np.broadcast_to(q_decay[:, :, None], full),
        np.broadcast_to(k_decay[:, :, None], full),
        np.broadcast_to(chunk_decay[:, None, None], full)))


def kernel(x, w_in, ret_gn_g, ret_gn_b, sgu_ln_g, sgu_ln_b, sgu_w, sgu_b, p_ret, p_sb, p_sgu,
           w_out, ln1_g, ln1_b, w_up, w_down, ln2_g, ln2_b):
    batch, seq, d = x.shape
    assert d == D_MODEL and seq % CHUNK == 0
    t = batch * seq
    tm = 512
    assert seq % tm == 0 and tm % SB_Q == 0 and SB_Q == SB_K
    cs, sn = _rotary_tables(seq)
    ret_tabs = _retention_tables()
    b_tab = jnp.broadcast_to(sgu_b[:, :, :, None], sgu_b.shape + (LANES,))
    post_weights = (p_ret, p_sb, p_sgu, w_out, w_up, w_down)
    x2 = x.reshape(t, d)
    w_proj = w_in[0].astype(BF16)
    for l in range(DEPTH):
        proj, ret, sb, weights = _in_proj(x2, w_proj, cs, sn, sgu_ln_g, sgu_ln_b, ret_tabs,
                                          ret_gn_g, ret_gn_b, post_weights, l, tm, seq)
        next_proj = (w_in,) if l + 1 < DEPTH else ()
        x2, cast = _post(x2, ret, sb, proj, sgu_w, b_tab, weights, ln1_g, ln1_b, ln2_g, ln2_b,
                         next_proj, l, tm)
        if cast:
            w_proj = cast[0]
    return x2.reshape(batch, seq, d)
```

```python
import functools

import jax
import jax.numpy as jnp
import numpy as np
from jax import lax
from jax.experimental import pallas as pl
from jax.experimental.pallas import tpu as pltpu

F32 = jnp.float32
BF16 = jnp.bfloat16

D_MODEL = 1024
DEPTH = 2
CHUNK = 128
RET_HEADS = 4
RET_HEAD_DIM = 128
SB_HEADS = 8
SB_HEAD_DIM = 64
SGU_GROUPS = 4
SEG = 512
D_FF = 4 * D_MODEL
N_IN = 9 * SEG + 3 * D_MODEL
ROPE_BASE = 10000.0
LN_EPS = 1e-5
DEEPNORM_ALPHA = (2 * DEPTH) ** 0.25

SEG_RQ, SEG_RK, SEG_RV, SEG_RG, SEG_SQ, SEG_SK, SEG_SV, SEG_GU, SEG_GV = range(9)
SEG_GATE0 = 9
N_SEG = N_IN // SEG
PROJ_SEG0 = SEG_GU
N_PROJ = (N_SEG - PROJ_SEG0) * SEG

LANES = 128
BF16_SUBLANES = 16
VMEM_LIMIT = 60 * 1024 * 1024


def _cparams(n_axes):
    return pltpu.CompilerParams(
        dimension_semantics=("arbitrary",) * n_axes,
        vmem_limit_bytes=VMEM_LIMIT)


def _resident(shape):
    zeros = (0,) * len(shape)
    return pl.BlockSpec(shape, lambda *_: zeros, pipeline_mode=pl.Buffered(1))


def _of_layer(shape, l):
    zeros = (0,) * len(shape)
    return pl.BlockSpec((None,) + tuple(shape), lambda *_: (l,) + zeros,
                        pipeline_mode=pl.Buffered(1))


def _cast_rider(params, l, n_steps):
    in_specs, out_specs, out_shapes = [], [], []
    for p in params:
        _, r, c = p.shape
        assert r % (n_steps * BF16_SUBLANES) == 0
        rb = r // n_steps
        in_specs.append(pl.BlockSpec((None, rb, c), lambda i: (l, i, 0)))
        out_specs.append(pl.BlockSpec((rb, c), lambda i: (i, 0)))
        out_shapes.append(jax.ShapeDtypeStruct((r, c), BF16))
    return in_specs, out_specs, out_shapes


def _run_casts(src_refs, dst_refs):
    for src, dst in zip(src_refs, dst_refs):
        dst[...] = src[...].astype(BF16)


def _layer_norm(x, g, b):
    mu = jnp.mean(x, axis=-1, keepdims=True)
    xc = x - mu
    var = jnp.mean(xc * xc, axis=-1, keepdims=True)
    return xc * lax.rsqrt(var + LN_EPS) * g + b


SB_Q = 256
SB_K = 256
SB_GROUPS = 2
LOG2E = 1.4426950408889634
SB_UNDERFLOW = 152.0
SB_MASKED = -1e30
SB_CLAMP = 126.0


class _SbUnit:
    def __init__(self, q_tile, k_ref, v_ref, acc_ref, run_ref, col0, qi, tri, causal, low):
        self.groups = range(SB_GROUPS)
        self.k_ref, self.v_ref, self.qi, self.tri, self.causal, self.low = (
            k_ref, v_ref, qi, tri, causal, low)
        self.acc_ref, self.run_ref = acc_ref, run_ref
        self.lanes = [slice(col0 + g * LANES, col0 + (g + 1) * LANES) for g in self.groups]
        self.qs = []
        for g in self.groups:
            q = q_tile[:, g * LANES:(g + 1) * LANES]
            zero = jnp.zeros_like(q)
            self.qs.append(jnp.concatenate(
                [jnp.where(low, q, zero), jnp.where(low, zero, q)], axis=0))

    @staticmethod
    def key_rows(kj):
        return pl.ds(pl.multiple_of(kj * SB_K, SB_K), SB_K)

    def scores(self, rows, g):
        return lax.dot_general(self.qs[g], self.k_ref[rows, self.lanes[g]],
                               (((1,), (1,)), ((), ())), preferred_element_type=F32)

    def suffix_sums(self, z):
        sp = jnp.maximum(jnp.log(1.0 + jnp.exp2(jnp.minimum(z, SB_CLAMP))) * LOG2E, z)
        return jnp.dot(sp.astype(BF16), self.tri, preferred_element_type=F32)

    def weighted(self, z, incl, run, rows, g):
        a = jnp.exp2(z - incl - run)
        return jnp.dot(a.astype(BF16), self.v_ref[rows, self.lanes[g]], preferred_element_type=F32)

    def stage_scores(self):
        self.rows_d = self.key_rows(self.qi)
        self.rows_p = self.key_rows(jnp.maximum(self.qi - 1, 0))
        self.z_d = [jnp.where(self.causal, self.scores(self.rows_d, g), SB_MASKED)
                    for g in self.groups]
        self.z_p = [self.scores(self.rows_p, g) for g in self.groups]

    def stage_sums(self):
        self.incl_d = [self.suffix_sums(z) for z in self.z_d]
        self.incl_p = [self.suffix_sums(z) for z in self.z_p]

    def stage_values(self):
        no_prev = jnp.where(self.qi == 0, -SB_MASKED, 0.0)
        run_p = [self.incl_d[g][:, 0:1] + no_prev for g in self.groups]
        run = [run_p[g] + self.incl_p[g][:, 0:1] for g in self.groups]
        for g in self.groups:
            self.acc_ref[g] = (self.weighted(self.z_d[g], self.incl_d[g], 0.0, self.rows_d, g)
                               + self.weighted(self.z_p[g], self.incl_p[g], run_p[g], self.rows_p, g))
            self.run_ref[g] = run[g]
        self.lowest_run = self.lowest(run)

    @staticmethod
    def lowest(run):
        return functools.reduce(jnp.minimum, [jnp.min(r) for r in run])

    def finish(self):
        qi = self.qi

        def live(c):
            return jnp.logical_and(c[0] < qi, c[1] <= SB_UNDERFLOW)

        def step(c):
            i = c[0]
            rows = self.key_rows(qi - 1 - i)
            z = [self.scores(rows, g) for g in self.groups]
            incl = [self.suffix_sums(z[g]) for g in self.groups]
            run = [self.run_ref[g] for g in self.groups]
            for g in self.groups:
                self.acc_ref[g] += self.weighted(z[g], incl[g], run[g], rows, g)
                run[g] = run[g] + incl[g][:, 0:1]
                self.run_ref[g] = run[g]
            return i + 1, self.lowest(run)

        @pl.when(live((jnp.int32(1), self.lowest_run)))
        def _():
            lax.while_loop(live, step, (jnp.int32(1), self.lowest_run))

        return jnp.concatenate(
            [jnp.where(self.low, self.acc_ref[g, :SB_Q], self.acc_ref[g, SB_Q:]).astype(BF16)
             for g in self.groups], axis=1)


def _inproj_kernel(n_casts, steps_per_seq, l, x_ref, w_ref, cs_ref, sn_ref, lng_ref, lnb_ref,
                   dint_ref, qd_ref, kd_ref, cd_ref, gng_ref, gnb_ref, *refs):
    o_ref, ret_ref, sb_ref = refs[n_casts:n_casts + 3]
    state_ref, ksc_ref, vsc_ref, acc_ref, run_ref = refs[-5:]
    _run_casts(refs[:n_casts], refs[n_casts + 3:-5])
    step_in_seq = pl.program_id(0) % steps_per_seq

    @pl.when(step_in_seq == 0)
    def _():
        state_ref[...] = jnp.zeros_like(state_ref)

    tm = x_ref.shape[0]
    xb = x_ref[...].astype(BF16)
    cs = cs_ref[...]
    sn = sn_ref[...]
    heads = range(RET_HEADS)
    n_chunks = tm // CHUNK
    chains = [(c, h) for c in range(n_chunks) for h in heads]
    hcols = [slice(h * RET_HEAD_DIM, (h + 1) * RET_HEAD_DIM) for h in heads]
    crows = [slice(c * CHUNK, (c + 1) * CHUNK) for c in range(n_chunks)]
    nt = (((1,), (1,)), ((), ()))
    tn = (((0,), (0,)), ((), ()))

    def project(j):
        return jnp.dot(xb, w_ref[:, j * SEG:(j + 1) * SEG], preferred_element_type=F32)

    def rotary(acc, scale):
        out = []
        for h in heads:
            xh = acc[:, hcols[h]]
            r = xh * cs + pltpu.roll(xh, RET_HEAD_DIM // 2, axis=1) * sn
            out.append((r * scale).astype(BF16) if scale is not None else r.astype(BF16))
        return out

    def emit(j):
        acc = project(j)
        if j == SEG_GU:
            res = jax.nn.gelu(acc)
        elif j == SEG_GV:
            res = _layer_norm(jax.nn.gelu(acc), lng_ref[l:l + 1, :], lnb_ref[l:l + 1, :])
        else:
            res = jax.nn.sigmoid(acc)
        o_ref[:, (j - PROJ_SEG0) * SEG:(j - PROJ_SEG0 + 1) * SEG] = res.astype(BF16)

    tile_rows = pl.ds(pl.multiple_of(step_in_seq * tm, tm), tm)
    sq = (project(SEG_SQ) * (SB_HEAD_DIM ** -0.5 * LOG2E)).astype(BF16)
    ksc_ref[tile_rows, :] = project(SEG_SK).astype(BF16)
    vsc_ref[tile_rows, :] = project(SEG_SV).astype(BF16)

    low = lax.broadcasted_iota(jnp.int32, (SB_Q, LANES), 1) < SB_HEAD_DIM
    tri = jnp.where(lax.broadcasted_iota(jnp.int32, (SB_K, SB_K), 0)
                    >= lax.broadcasted_iota(jnp.int32, (SB_K, SB_K), 1), 1.0, 0.0).astype(BF16)
    causal = (lax.broadcasted_iota(jnp.int32, (2 * SB_Q, SB_K), 1)
              < (lax.broadcasted_iota(jnp.int32, (2 * SB_Q, SB_K), 0) & (SB_Q - 1)))
    unit_cols = SB_GROUPS * LANES
    units = []
    for qb in range(tm // SB_Q):
        for u in range(SEG // unit_cols):
            q_tile = sq[qb * SB_Q:(qb + 1) * SB_Q, u * unit_cols:(u + 1) * unit_cols]
            units.append((qb, u, _SbUnit(q_tile, ksc_ref, vsc_ref, acc_ref.at[len(units)],
                                         run_ref.at[len(units)], u * unit_cols,
                                         step_in_seq * (tm // SB_Q) + qb, tri, causal, low)))
    sb_stages = [stage for _, _, unit in units
                 for stage in (unit.stage_scores, unit.stage_sums, unit.stage_values)]

    def sb_advance(n):
        for _ in range(n):
            if sb_stages:
                sb_stages.pop(0)()

    sb_advance(1)
    rq = rotary(project(SEG_RQ), None)
    sb_advance(1)
    rk = rotary(project(SEG_RK), RET_HEAD_DIM ** -0.5)
    sb_advance(1)
    rv = project(SEG_RV).astype(BF16)
    sb_advance(1)
    rg = jax.nn.silu(project(SEG_RG)).astype(BF16)
    q = {(c, h): rq[h][crows[c]] for c, h in chains}
    k = {(c, h): rk[h][crows[c]] for c, h in chains}
    v = {(c, h): rv[crows[c], hcols[h]] for c, h in chains}
    s = {(c, h): lax.dot_general(q[c, h], k[c, h], nt, preferred_element_type=F32) * dint_ref[h]
         for c, h in chains}
    kdec = {(c, h): (k[c, h].astype(F32) * kd_ref[h]).astype(BF16) for c, h in chains}
    kv = {(c, h): lax.dot_general(kdec[c, h], v[c, h], tn, preferred_element_type=F32)
          for c, h in chains}
    qdec = {(c, h): (q[c, h].astype(F32) * qd_ref[h]).astype(BF16) for c, h in chains}
    sb_advance(1)
    emit(SEG_GU)
    sb_advance(1)
    st = {(0, h): state_ref[h] for h in heads}
    for c in range(n_chunks):
        for h in heads:
            st[c + 1, h] = st[c, h] * cd_ref[h] + kv[c, h]
    for h in heads:
        state_ref[h] = st[n_chunks, h]
    ret = {(c, h): jnp.dot(jnp.concatenate([s[c, h].astype(BF16), qdec[c, h]], axis=1),
                           jnp.concatenate([v[c, h], st[c, h].astype(BF16)], axis=0),
                           preferred_element_type=F32)
           for c, h in chains}
    emit(SEG_GV)
    sb_advance(1)
    for c, h in chains:
        y = _layer_norm(ret[c, h], gng_ref[l:l + 1, hcols[h]], gnb_ref[l:l + 1, hcols[h]])
        ret_ref[crows[c], hcols[h]] = (rg[crows[c], hcols[h]].astype(F32) * y).astype(BF16)
    for j in range(SEG_GATE0, N_SEG):
        emit(j)
        sb_advance(1)
    sb_advance(len(sb_stages))
    for qb, u, unit in units:
        sb_ref[qb * SB_Q:(qb + 1) * SB_Q, u * unit_cols:(u + 1) * unit_cols] = unit.finish()


def _in_proj(x2, w, cs, sn, lng, lnb, ret_tabs, gng, gnb, to_cast, l, tm, seq):
    t = x2.shape[0]
    pos_blocks = seq // tm
    cast_in, cast_out, cast_shapes = _cast_rider(to_cast, l, t // tm)
    tab_spec = _resident((RET_HEADS, CHUNK, RET_HEAD_DIM))
    n_units = (tm // SB_Q) * (SEG // (SB_GROUPS * LANES))
    out = pl.pallas_call(
        functools.partial(_inproj_kernel, len(to_cast), pos_blocks, l),
        grid=(t // tm,),
        in_specs=[
            pl.BlockSpec((tm, D_MODEL), lambda i: (i, 0)),
            _resident((D_MODEL, N_IN)),
            pl.BlockSpec((tm, LANES), lambda i: (i % pos_blocks, 0)),
            pl.BlockSpec((tm, LANES), lambda i: (i % pos_blocks, 0)),
            _resident(lng.shape),
            _resident(lnb.shape),
            tab_spec, tab_spec, tab_spec, tab_spec,
            _resident(gng.shape),
            _resident(gnb.shape),
        ] + cast_in,
        out_specs=[pl.BlockSpec((tm, N_PROJ), lambda i: (i, 0)),
                   pl.BlockSpec((tm, SEG), lambda i: (i, 0)),
                   pl.BlockSpec((tm, SEG), lambda i: (i, 0))] + cast_out,
        out_shape=[jax.ShapeDtypeStruct((t, N_PROJ), BF16),
                   jax.ShapeDtypeStruct((t, SEG), BF16),
                   jax.ShapeDtypeStruct((t, SEG), BF16)] + cast_shapes,
        scratch_shapes=[pltpu.VMEM((RET_HEADS, RET_HEAD_DIM, RET_HEAD_DIM), F32),
                        pltpu.VMEM((seq, SEG), BF16), pltpu.VMEM((seq, SEG), BF16),
                        pltpu.VMEM((n_units, SB_GROUPS, 2 * SB_Q, LANES), F32),
                        pltpu.VMEM((n_units, SB_GROUPS, 2 * SB_Q, 1), F32)],
        compiler_params=_cparams(1),
        name="in_proj",
    )(x2, w, cs, sn, lng, lnb, *ret_tabs, gng, gnb, *to_cast)
    return out[0], out[1], out[2], out[3:]


FF_CHUNK = 1024
POST_PARTS = 2


def _post_kernel(n_casts, l, x_ref, ret_ref, sb_ref, gu_ref, gv_ref, g0, g1, g2, g3, g4, g5,
                 sw_ref, sbias_ref, pr_ref, ps_ref, pg_ref, wo_ref, ln1g_ref, ln1b_ref,
                 wu_ref, wd_ref, ln2g_ref, ln2b_ref, *refs):
    o_ref = refs[n_casts]
    sg_ref = refs[-1]
    _run_casts(refs[:n_casts], refs[n_casts + 1:-1])
    row = lax.broadcasted_iota(jnp.int32, (CHUNK, CHUNK), 0)
    col = lax.broadcasted_iota(jnp.int32, (CHUNK, CHUNK), 1)
    causal = row >= col
    for g in range(SGU_GROUPS):
        wm = jnp.where(causal, sw_ref[g], 0.0).astype(BF16)
        bias = sbias_ref[g]
        gc = slice(g * LANES, (g + 1) * LANES)
        for c in range(x_ref.shape[0] // CHUNK):
            rc = slice(c * CHUNK, (c + 1) * CHUNK)
            sv = jnp.dot(wm, gv_ref[rc, gc], preferred_element_type=F32) + bias
            sg_ref[rc, gc] = (gu_ref[rc, gc].astype(F32) * sv).astype(BF16)

    gates = ((g0, g1), (g2, g3), (g4, g5))
    branches = ((ret_ref, pr_ref), (sb_ref, ps_ref), (sg_ref, pg_ref))
    parts = range(POST_PARTS)
    part_rows = x_ref.shape[0] // POST_PARTS
    rows = [slice(p * part_rows, (p + 1) * part_rows) for p in parts]
    merged = []
    for p in parts:
        halves = []
        for half in range(D_MODEL // SEG):
            cols = slice(half * SEG, (half + 1) * SEG)
            m = None
            for (a_ref, p_ref), gate in zip(branches, gates):
                term = gate[half][rows[p], :].astype(F32) * jnp.dot(
                    a_ref[rows[p], :], p_ref[:, cols], preferred_element_type=F32)
                m = term if m is None else m + term
            halves.append(m.astype(BF16))
        merged.append(halves)
    y = [jnp.dot(merged[p][0], wo_ref[:SEG, :], preferred_element_type=F32)
         + jnp.dot(merged[p][1], wo_ref[SEG:, :], preferred_element_type=F32) for p in parts]
    x1 = [_layer_norm(DEEPNORM_ALPHA * x_ref[rows[p], :] + y[p],
                      ln1g_ref[l:l + 1, :], ln1b_ref[l:l + 1, :]) for p in parts]

    xb = [x1[p].astype(BF16) for p in parts]
    acc = [None] * POST_PARTS
    for c in range(D_FF // FF_CHUNK):
        cols = slice(c * FF_CHUNK, (c + 1) * FF_CHUNK)
        h = [jnp.maximum(jnp.dot(xb[p], wu_ref[:, cols], preferred_element_type=F32), 0.0)
             for p in parts]
        for p in parts:
            part = jnp.dot((h[p] * h[p]).astype(BF16), wd_ref[cols, :], preferred_element_type=F32)
            acc[p] = part if acc[p] is None else acc[p] + part
    for p in parts:
        o_ref[rows[p], :] = _layer_norm(DEEPNORM_ALPHA * x1[p] + acc[p],
                                        ln2g_ref[l:l + 1, :], ln2b_ref[l:l + 1, :])


def _post(x2, ret, sb, proj, sgu_w, b_tab, weights, ln1g, ln1b, ln2g, ln2b, to_cast, l, tm):
    t = x2.shape[0]
    p_ret, p_sb, p_sgu, w_out, w_up, w_down = weights
    row_spec = lambda w: pl.BlockSpec((tm, w), lambda i: (i, 0))
    seg_spec = lambda seg: pl.BlockSpec((tm, SEG), lambda i: (i, seg - PROJ_SEG0))
    cast_in, cast_out, cast_shapes = _cast_rider(to_cast, l + 1, t // tm)
    out = pl.pallas_call(
        functools.partial(_post_kernel, len(to_cast), l),
        grid=(t // tm,),
        in_specs=[row_spec(D_MODEL), row_spec(SEG), row_spec(SEG),
                  seg_spec(SEG_GU), seg_spec(SEG_GV)]
        + [seg_spec(SEG_GATE0 + j) for j in range(6)]
        + [_of_layer((SGU_GROUPS, CHUNK, CHUNK), l), _of_layer((SGU_GROUPS, CHUNK, LANES), l)]
        + [_resident((SEG, D_MODEL))] * 3
        + [_resident((D_MODEL, D_MODEL)), _resident(ln1g.shape), _resident(ln1b.shape),
           _resident((D_MODEL, D_FF)), _resident((D_FF, D_MODEL)),
           _resident(ln2g.shape), _resident(ln2b.shape)]
        + cast_in,
        out_specs=[row_spec(D_MODEL)] + cast_out,
        out_shape=[jax.ShapeDtypeStruct((t, D_MODEL), F32)] + cast_shapes,
        scratch_shapes=[pltpu.VMEM((tm, SEG), BF16)],
        compiler_params=_cparams(1),
        name="post_mixer",
    )(x2, ret, sb, *([proj] * 8), sgu_w, b_tab, p_ret, p_sb, p_sgu, w_out, ln1g, ln1b,
      w_up, w_down, ln2g, ln2b, *to_cast)
    return out[0], out[1:]


def _rotary_tables(seq):
    half = RET_HEAD_DIM // 2
    inv_freq = ROPE_BASE ** (-np.arange(half, dtype=np.float64) / half)
    ang = np.arange(seq, dtype=np.float64)[:, None] * inv_freq[None, :]
    cos, sin = np.cos(ang), np.sin(ang)
    return (np.concatenate([cos, cos], axis=1).astype(np.float32),
            np.concatenate([-sin, sin], axis=1).astype(np.float32))


def _retention_tables():
    log_g = np.log(1.0 - 2.0 ** (-5.0 - np.arange(RET_HEADS, dtype=np.float64)))
    idx = np.arange(CHUNK, dtype=np.float64)
    diff = idx[:, None] - idx[None, :]
    dint = np.where(diff[None] >= 0, np.exp(log_g[:, None, None] * diff[None]), 0.0)
    k_decay = np.exp(log_g[:, None] * (CHUNK - 1 - idx)[None, :])
    q_decay = np.exp(log_g[:, None] * (idx + 1.0)[None, :])
    chunk_decay = np.exp(log_g * CHUNK)
    full = (RET_HEADS, CHUNK, RET_HEAD_DIM)
    return tuple(np.ascontiguousarray(a, dtype=np.float32) for a in (
        dint,
        np.broadcast_to(q_decay[:, :, None], full),
        np.broadcast_to(k_decay[:, :, None], full),
        np.broadcast_to(chunk_decay[:, None, None], full)))


def kernel(x, w_in, ret_gn_g, ret_gn_b, sgu_ln_g, sgu_ln_b, sgu_w, sgu_b, p_ret, p_sb, p_sgu,
           w_out, ln1_g, ln1_b, w_up, w_down, ln2_g, ln2_b):
    batch, seq, d = x.shape
    assert d == D_MODEL and seq % CHUNK == 0
    t = batch * seq
    tm = 512
    assert seq % tm == 0 and tm % SB_Q == 0 and SB_Q == SB_K
    cs, sn = _rotary_tables(seq)
    ret_tabs = _retention_tables()
    b_tab = jnp.broadcast_to(sgu_b[:, :, :, None], sgu_b.shape + (LANES,))
    post_weights = (p_ret, p_sb, p_sgu, w_out, w_up, w_down)
    x2 = x.reshape(t, d)
    w_proj = w_in[0].astype(BF16)
    for l in range(DEPTH):
        proj, ret, sb, weights = _in_proj(x2, w_proj, cs, sn, sgu_ln_g, sgu_ln_b, ret_tabs,
                                          ret_gn_g, ret_gn_b, post_weights, l, tm, seq)
        next_proj = (w_in,) if l + 1 < DEPTH else ()
        x2, cast = _post(x2, ret, sb, proj, sgu_w, b_tab, weights, ln1_g, ln1_b, ln2_g, ln2_b,
                         next_proj, l, tm)
        if cast:
            w_proj = cast[0]
    return x2.reshape(batch, seq, d)
```

```python
import functools

import jax
import jax.numpy as jnp
import numpy as np
from jax import lax
from jax.experimental import pallas as pl
from jax.experimental.pallas import tpu as pltpu

F32 = jnp.float32
BF16 = jnp.bfloat16

D_MODEL = 1024
DEPTH = 2
CHUNK = 128
RET_HEADS = 4
RET_HEAD_DIM = 128
SB_HEADS = 8
SB_HEAD_DIM = 64
SGU_GROUPS = 4
SEG = 512
D_FF = 4 * D_MODEL
N_IN = 9 * SEG + 3 * D_MODEL
ROPE_BASE = 10000.0
LN_EPS = 1e-5
DEEPNORM_ALPHA = (2 * DEPTH) ** 0.25

SEG_RQ, SEG_RK, SEG_RV, SEG_RG, SEG_SQ, SEG_SK, SEG_SV, SEG_GU, SEG_GV = range(9)
SEG_GATE0 = 9
N_SEG = N_IN // SEG
PROJ_SEG0 = SEG_GU
N_PROJ = (N_SEG - PROJ_SEG0) * SEG

LANES = 128
BF16_SUBLANES = 16
VMEM_LIMIT = 60 * 1024 * 1024


def _cparams(n_axes):
    return pltpu.CompilerParams(
        dimension_semantics=("arbitrary",) * n_axes,
        vmem_limit_bytes=VMEM_LIMIT)


def _resident(shape):
    zeros = (0,) * len(shape)
    return pl.BlockSpec(shape, lambda *_: zeros, pipeline_mode=pl.Buffered(1))


def _of_layer(shape, l):
    zeros = (0,) * len(shape)
    return pl.BlockSpec((None,) + tuple(shape), lambda *_: (l,) + zeros,
                        pipeline_mode=pl.Buffered(1))


def _cast_rider(params, l, n_steps):
    in_specs, out_specs, out_shapes = [], [], []
    for p in params:
        _, r, c = p.shape
        assert r % (n_steps * BF16_SUBLANES) == 0
        rb = r // n_steps
        in_specs.append(pl.BlockSpec((None, rb, c), lambda i: (l, i, 0)))
        out_specs.append(pl.BlockSpec((rb, c), lambda i: (i, 0)))
        out_shapes.append(jax.ShapeDtypeStruct((r, c), BF16))
    return in_specs, out_specs, out_shapes


def _run_casts(src_refs, dst_refs):
    for src, dst in zip(src_refs, dst_refs):
        dst[...] = src[...].astype(BF16)


def _layer_norm(x, g, b):
    mu = jnp.mean(x, axis=-1, keepdims=True)
    xc = x - mu
    var = jnp.mean(xc * xc, axis=-1, keepdims=True)
    return xc * lax.rsqrt(var + LN_EPS) * g + b


SB_Q = 256
SB_K = 256
SB_GROUPS = 2
LOG2E = 1.4426950408889634
SB_UNDERFLOW = 152.0
SB_MASKED = -1e30
SB_CLAMP = 126.0


class _SbUnit:
    def __init__(self, q_tile, k_ref, v_ref, acc_ref, run_ref, col0, qi, tri, causal, low):
        self.groups = range(SB_GROUPS)
        self.k_ref, self.v_ref, self.qi, self.tri, self.causal, self.low = (
            k_ref, v_ref, qi, tri, causal, low)
        self.acc_ref, self.run_ref = acc_ref, run_ref
        self.lanes = [slice(col0 + g * LANES, col0 + (g + 1) * LANES) for g in self.groups]
        self.qs = []
        for g in self.groups:
            q = q_tile[:, g * LANES:(g + 1) * LANES]
            zero = jnp.zeros_like(q)
            self.qs.append(jnp.concatenate(
                [jnp.where(low, q, zero), jnp.where(low, zero, q)], axis=0))

    @staticmethod
    def key_rows(kj):
        return pl.ds(pl.multiple_of(kj * SB_K, SB_K), SB_K)

    def scores(self, rows, g):
        return lax.dot_general(self.qs[g], self.k_ref[rows, self.lanes[g]],
                               (((1,), (1,)), ((), ())), preferred_element_type=F32)

    def log_terms(self, z):
        sp = jnp.maximum(jnp.log(1.0 + jnp.exp2(jnp.minimum(z, SB_CLAMP))) * LOG2E, z)
        sp16 = sp.astype(BF16)
        later = jnp.dot(sp16, self.tri, preferred_element_type=F32)
        return z - sp, later, later[:, 0:1] + sp16[:, 0:1].astype(F32)

    def weighted(self, terms, run, rows, g):
        log_sig, later, _ = terms
        a = jnp.exp2(log_sig - later - run)
        return jnp.dot(a.astype(BF16), self.v_ref[rows, self.lanes[g]], preferred_element_type=F32)

    def stage_scores(self):
        self.rows_d = self.key_rows(self.qi)
        self.rows_p = self.key_rows(jnp.maximum(self.qi - 1, 0))
        self.z_d = [jnp.where(self.causal, self.scores(self.rows_d, g), SB_MASKED)
                    for g in self.groups]
        self.z_p = [self.scores(self.rows_p, g) for g in self.groups]

    def stage_sums(self):
        self.terms_d = [self.log_terms(z) for z in self.z_d]
        self.terms_p = [self.log_terms(z) for z in self.z_p]

    def stage_values(self):
        no_prev = jnp.where(self.qi == 0, -SB_MASKED, 0.0)
        run_p = [self.terms_d[g][2] + no_prev for g in self.groups]
        run = [run_p[g] + self.terms_p[g][2] for g in self.groups]
        for g in self.groups:
            self.acc_ref[g] = (self.weighted(self.terms_d[g], 0.0, self.rows_d, g)
                               + self.weighted(self.terms_p[g], run_p[g], self.rows_p, g))
            self.run_ref[g] = run[g]
        self.lowest_run = self.lowest(run)

    @staticmethod
    def lowest(run):
        return functools.reduce(jnp.minimum, [jnp.min(r) for r in run])

    def finish(self):
        qi = self.qi

        def live(c):
            return jnp.logical_and(c[0] < qi, c[1] <= SB_UNDERFLOW)

        def step(c):
            i = c[0]
            rows = self.key_rows(qi - 1 - i)
            terms = [self.log_terms(self.scores(rows, g)) for g in self.groups]
            run = [self.run_ref[g] for g in self.groups]
            for g in self.groups:
                self.acc_ref[g] += self.weighted(terms[g], run[g], rows, g)
                run[g] = run[g] + terms[g][2]
                self.run_ref[g] = run[g]
            return i + 1, self.lowest(run)

        @pl.when(live((jnp.int32(1), self.lowest_run)))
        def _():
            lax.while_loop(live, step, (jnp.int32(1), self.lowest_run))

        return jnp.concatenate(
            [jnp.where(self.low, self.acc_ref[g, :SB_Q], self.acc_ref[g, SB_Q:]).astype(BF16)
             for g in self.groups], axis=1)


def _inproj_kernel(n_casts, steps_per_seq, l, x_ref, w_ref, cs_ref, sn_ref, lng_ref, lnb_ref,
                   dint_ref, qd_ref, kd_ref, cd_ref, gng_ref, gnb_ref, *refs):
    o_ref, ret_ref, sb_ref = refs[n_casts:n_casts + 3]
    state_ref, ksc_ref, vsc_ref, acc_ref, run_ref = refs[-5:]
    _run_casts(refs[:n_casts], refs[n_casts + 3:-5])
    step_in_seq = pl.program_id(0) % steps_per_seq

    @pl.when(step_in_seq == 0)
    def _():
        state_ref[...] = jnp.zeros_like(state_ref)

    tm = x_ref.shape[0]
    xb = x_ref[...].astype(BF16)
    cs = cs_ref[...]
    sn = sn_ref[...]
    heads = range(RET_HEADS)
    n_chunks = tm // CHUNK
    chains = [(c, h) for c in range(n_chunks) for h in heads]
    hcols = [slice(h * RET_HEAD_DIM, (h + 1) * RET_HEAD_DIM) for h in heads]
    crows = [slice(c * CHUNK, (c + 1) * CHUNK) for c in range(n_chunks)]
    nt = (((1,), (1,)), ((), ()))
    tn = (((0,), (0,)), ((), ()))

    def project(j):
        return jnp.dot(xb, w_ref[:, j * SEG:(j + 1) * SEG], preferred_element_type=F32)

    def rotary(acc, scale):
        out = []
        for h in heads:
            xh = acc[:, hcols[h]]
            r = xh * cs + pltpu.roll(xh, RET_HEAD_DIM // 2, axis=1) * sn
            out.append((r * scale).astype(BF16) if scale is not None else r.astype(BF16))
        return out

    def emit(j):
        acc = project(j)
        if j == SEG_GU:
            res = jax.nn.gelu(acc)
        elif j == SEG_GV:
            res = _layer_norm(jax.nn.gelu(acc), lng_ref[l:l + 1, :], lnb_ref[l:l + 1, :])
        else:
            res = jax.nn.sigmoid(acc)
        o_ref[:, (j - PROJ_SEG0) * SEG:(j - PROJ_SEG0 + 1) * SEG] = res.astype(BF16)

    tile_rows = pl.ds(pl.multiple_of(step_in_seq * tm, tm), tm)
    sq = (project(SEG_SQ) * (SB_HEAD_DIM ** -0.5 * LOG2E)).astype(BF16)
    ksc_ref[tile_rows, :] = project(SEG_SK).astype(BF16)
    vsc_ref[tile_rows, :] = project(SEG_SV).astype(BF16)

    low = lax.broadcasted_iota(jnp.int32, (SB_Q, LANES), 1) < SB_HEAD_DIM
    tri = jnp.where(lax.broadcasted_iota(jnp.int32, (SB_K, SB_K), 0)
                    > lax.broadcasted_iota(jnp.int32, (SB_K, SB_K), 1), 1.0, 0.0).astype(BF16)
    causal = (lax.broadcasted_iota(jnp.int32, (2 * SB_Q, SB_K), 1)
              < (lax.broadcasted_iota(jnp.int32, (2 * SB_Q, SB_K), 0) & (SB_Q - 1)))
    unit_cols = SB_GROUPS * LANES
    units = []
    for qb in range(tm // SB_Q):
        for u in range(SEG // unit_cols):
            q_tile = sq[qb * SB_Q:(qb + 1) * SB_Q, u * unit_cols:(u + 1) * unit_cols]
            units.append((qb, u, _SbUnit(q_tile, ksc_ref, vsc_ref, acc_ref.at[len(units)],
                                         run_ref.at[len(units)], u * unit_cols,
                                         step_in_seq * (tm // SB_Q) + qb, tri, causal, low)))
    sb_stages = [stage for _, _, unit in units
                 for stage in (unit.stage_scores, unit.stage_sums, unit.stage_values)]

    def sb_advance(n):
        for _ in range(n):
            if sb_stages:
                sb_stages.pop(0)()

    sb_advance(1)
    rq = rotary(project(SEG_RQ), None)
    sb_advance(1)
    rk = rotary(project(SEG_RK), RET_HEAD_DIM ** -0.5)
    sb_advance(1)
    rv = project(SEG_RV).astype(BF16)
    sb_advance(1)
    rg = jax.nn.silu(project(SEG_RG)).astype(BF16)
    q = {(c, h): rq[h][crows[c]] for c, h in chains}
    k = {(c, h): rk[h][crows[c]] for c, h in chains}
    v = {(c, h): rv[crows[c], hcols[h]] for c, h in chains}
    s = {(c, h): lax.dot_general(q[c, h], k[c, h], nt, preferred_element_type=F32) * dint_ref[h]
         for c, h in chains}
    kdec = {(c, h): (k[c, h].astype(F32) * kd_ref[h]).astype(BF16) for c, h in chains}
    kv = {(c, h): lax.dot_general(kdec[c, h], v[c, h], tn, preferred_element_type=F32)
          for c, h in chains}
    qdec = {(c, h): (q[c, h].astype(F32) * qd_ref[h]).astype(BF16) for c, h in chains}
    sb_advance(1)
    emit(SEG_GU)
    sb_advance(1)
    st = {(0, h): state_ref[h] for h in heads}
    for c in range(n_chunks):
        for h in heads:
            st[c + 1, h] = st[c, h] * cd_ref[h] + kv[c, h]
    for h in heads:
        state_ref[h] = st[n_chunks, h]
    ret = {(c, h): jnp.dot(jnp.concatenate([s[c, h].astype(BF16), qdec[c, h]], axis=1),
                           jnp.concatenate([v[c, h], st[c, h].astype(BF16)], axis=0),
                           preferred_element_type=F32)
           for c, h in chains}
    emit(SEG_GV)
    sb_advance(1)
    for c, h in chains:
        y = _layer_norm(ret[c, h], gng_ref[l:l + 1, hcols[h]], gnb_ref[l:l + 1, hcols[h]])
        ret_ref[crows[c], hcols[h]] = (rg[crows[c], hcols[h]].astype(F32) * y).astype(BF16)
    for j in range(SEG_GATE0, N_SEG):
        emit(j)
        sb_advance(1)
    sb_advance(len(sb_stages))
    for qb, u, unit in units:
        sb_ref[qb * SB_Q:(qb + 1) * SB_Q, u * unit_cols:(u + 1) * unit_cols] = unit.finish()


def _in_proj(x2, w, cs, sn, lng, lnb, ret_tabs, gng, gnb, to_cast, l, tm, seq):
    t = x2.shape[0]
    pos_blocks = seq // tm
    cast_in, cast_out, cast_shapes = _cast_rider(to_cast, l, t // tm)
    tab_spec = _resident((RET_HEADS, CHUNK, RET_HEAD_DIM))
    n_units = (tm // SB_Q) * (SEG // (SB_GROUPS * LANES))
    out = pl.pallas_call(
        functools.partial(_inproj_kernel, len(to_cast), pos_blocks, l),
        grid=(t // tm,),
        in_specs=[
            pl.BlockSpec((tm, D_MODEL), lambda i: (i, 0)),
            _resident((D_MODEL, N_IN)),
            pl.BlockSpec((tm, LANES), lambda i: (i % pos_blocks, 0)),
            pl.BlockSpec((tm, LANES), lambda i: (i % pos_blocks, 0)),
            _resident(lng.shape),
            _resident(lnb.shape),
            tab_spec, tab_spec, tab_spec, tab_spec,
            _resident(gng.shape),
            _resident(gnb.shape),
        ] + cast_in,
        out_specs=[pl.BlockSpec((tm, N_PROJ), lambda i: (i, 0)),
                   pl.BlockSpec((tm, SEG), lambda i: (i, 0)),
                   pl.BlockSpec((tm, SEG), lambda i: (i, 0))] + cast_out,
        out_shape=[jax.ShapeDtypeStruct((t, N_PROJ), BF16),
                   jax.ShapeDtypeStruct((t, SEG), BF16),
                   jax.ShapeDtypeStruct((t, SEG), BF16)] + cast_shapes,
        scratch_shapes=[pltpu.VMEM((RET_HEADS, RET_HEAD_DIM, RET_HEAD_DIM), F32),
                        pltpu.VMEM((seq, SEG), BF16), pltpu.VMEM((seq, SEG), BF16),
                        pltpu.VMEM((n_units, SB_GROUPS, 2 * SB_Q, LANES), F32),
                        pltpu.VMEM((n_units, SB_GROUPS, 2 * SB_Q, 1), F32)],
        compiler_params=_cparams(1),
        name="in_proj",
    )(x2, w, cs, sn, lng, lnb, *ret_tabs, gng, gnb, *to_cast)
    return out[0], out[1], out[2], out[3:]


FF_CHUNK = 1024
POST_PARTS = 2


def _post_kernel(n_casts, l, x_ref, ret_ref, sb_ref, gu_ref, gv_ref, g0, g1, g2, g3, g4, g5,
                 sw_ref, sbias_ref, pr_ref, ps_ref, pg_ref, wo_ref, ln1g_ref, ln1b_ref,
                 wu_ref, wd_ref, ln2g_ref, ln2b_ref, *refs):
    o_ref = refs[n_casts]
    sg_ref = refs[-1]
    _run_casts(refs[:n_casts], refs[n_casts + 1:-1])
    row = lax.broadcasted_iota(jnp.int32, (CHUNK, CHUNK), 0)
    col = lax.broadcasted_iota(jnp.int32, (CHUNK, CHUNK), 1)
    causal = row >= col
    for g in range(SGU_GROUPS):
        wm = jnp.where(causal, sw_ref[g], 0.0).astype(BF16)
        bias = sbias_ref[g]
        gc = slice(g * LANES, (g + 1) * LANES)
        for c in range(x_ref.shape[0] // CHUNK):
            rc = slice(c * CHUNK, (c + 1) * CHUNK)
            sv = jnp.dot(wm, gv_ref[rc, gc], preferred_element_type=F32) + bias
            sg_ref[rc, gc] = (gu_ref[rc, gc].astype(F32) * sv).astype(BF16)

    gates = ((g0, g1), (g2, g3), (g4, g5))
    branches = ((ret_ref, pr_ref), (sb_ref, ps_ref), (sg_ref, pg_ref))
    parts = range(POST_PARTS)
    part_rows = x_ref.shape[0] // POST_PARTS
    rows = [slice(p * part_rows, (p + 1) * part_rows) for p in parts]
    merged = []
    for p in parts:
        halves = []
        for half in range(D_MODEL // SEG):
            cols = slice(half * SEG, (half + 1) * SEG)
            m = None
            for (a_ref, p_ref), gate in zip(branches, gates):
                term = gate[half][rows[p], :].astype(F32) * jnp.dot(
                    a_ref[rows[p], :], p_ref[:, cols], preferred_element_type=F32)
                m = term if m is None else m + term
            halves.append(m.astype(BF16))
        merged.append(halves)
    y = [jnp.dot(merged[p][0], wo_ref[:SEG, :], preferred_element_type=F32)
         + jnp.dot(merged[p][1], wo_ref[SEG:, :], preferred_element_type=F32) for p in parts]
    x1 = [_layer_norm(DEEPNORM_ALPHA * x_ref[rows[p], :] + y[p],
                      ln1g_ref[l:l + 1, :], ln1b_ref[l:l + 1, :]) for p in parts]

    xb = [x1[p].astype(BF16) for p in parts]
    acc = [None] * POST_PARTS
    for c in range(D_FF // FF_CHUNK):
        cols = slice(c * FF_CHUNK, (c + 1) * FF_CHUNK)
        h = [jnp.maximum(jnp.dot(xb[p], wu_ref[:, cols], preferred_element_type=F32), 0.0)
             for p in parts]
        for p in parts:
            part = jnp.dot((h[p] * h[p]).astype(BF16), wd_ref[cols, :], preferred_element_type=F32)
            acc[p] = part if acc[p] is None else acc[p] + part
    for p in parts:
        o_ref[rows[p], :] = _layer_norm(DEEPNORM_ALPHA * x1[p] + acc[p],
                                        ln2g_ref[l:l + 1, :], ln2b_ref[l:l + 1, :])


def _post(x2, ret, sb, proj, sgu_w, b_tab, weights, ln1g, ln1b, ln2g, ln2b, to_cast, l, tm):
    t = x2.shape[0]
    p_ret, p_sb, p_sgu, w_out, w_up, w_down = weights
    row_spec = lambda w: pl.BlockSpec((tm, w), lambda i: (i, 0))
    seg_spec = lambda seg: pl.BlockSpec((tm, SEG), lambda i: (i, seg - PROJ_SEG0))
    cast_in, cast_out, cast_shapes = _cast_rider(to_cast, l + 1, t // tm)
    out = pl.pallas_call(
        functools.partial(_post_kernel, len(to_cast), l),
        grid=(t // tm,),
        in_specs=[row_spec(D_MODEL), row_spec(SEG), row_spec(SEG),
                  seg_spec(SEG_GU), seg_spec(SEG_GV)]
        + [seg_spec(SEG_GATE0 + j) for j in range(6)]
        + [_of_layer((SGU_GROUPS, CHUNK, CHUNK), l), _of_layer((SGU_GROUPS, CHUNK, LANES), l)]
        + [_resident((SEG, D_MODEL))] * 3
        + [_resident((D_MODEL, D_MODEL)), _resident(ln1g.shape), _resident(ln1b.shape),
           _resident((D_MODEL, D_FF)), _resident((D_FF, D_MODEL)),
           _resident(ln2g.shape), _resident(ln2b.shape)]
        + cast_in,
        out_specs=[row_spec(D_MODEL)] + cast_out,
        out_shape=[jax.ShapeDtypeStruct((t, D_MODEL), F32)] + cast_shapes,
        scratch_shapes=[pltpu.VMEM((tm, SEG), BF16)],
        compiler_params=_cparams(1),
        name="post_mixer",
    )(x2, ret, sb, *([proj] * 8), sgu_w, b_tab, p_ret, p_sb, p_sgu, w_out, ln1g, ln1b,
      w_up, w_down, ln2g, ln2b, *to_cast)
    return out[0], out[1:]


def _rotary_tables(seq):
    half = RET_HEAD_DIM // 2
    inv_freq = ROPE_BASE ** (-np.arange(half, dtype=np.float64) / half)
    ang = np.arange(seq, dtype=np.float64)[:, None] * inv_freq[None, :]
    cos, sin = np.cos(ang), np.sin(ang)
    return (np.concatenate([cos, cos], axis=1).astype(np.float32),
            np.concatenate([-sin, sin], axis=1).astype(np.float32))


def _retention_tables():
    log_g = np.log(1.0 - 2.0 ** (-5.0 - np.arange(RET_HEADS, dtype=np.float64)))
    idx = np.arange(CHUNK, dtype=np.float64)
    diff = idx[:, None] - idx[None, :]
    dint = np.where(diff[None] >= 0, np.exp(log_g[:, None, None] * diff[None]), 0.0)
    k_decay = np.exp(log_g[:, None] * (CHUNK - 1 - idx)[None, :])
    q_decay = np.exp(log_g[:, None] * (idx + 1.0)[None, :])
    chunk_decay = np.exp(log_g * CHUNK)
    full = (RET_HEADS, CHUNK, RET_HEAD_DIM)
    return tuple(np.ascontiguousarray(a, dtype=np.float32) for a in (
        dint,
        np.broadcast_to(q_decay[:, :, None], full),
        np.broadcast_to(k_decay[:, :, None], full),
        np.broadcast_to(chunk_decay[:, None, None], full)))


def kernel(x, w_in, ret_gn_g, ret_gn_b, sgu_ln_g, sgu_ln_b, sgu_w, sgu_b, p_ret, p_sb, p_sgu,
           w_out, ln1_g, ln1_b, w_up, w_down, ln2_g, ln2_b):
    batch, seq, d = x.shape
    assert d == D_MODEL and seq % CHUNK == 0
    t = batch * seq
    tm = 512
    assert seq % tm == 0 and tm % SB_Q == 0 and SB_Q == SB_K and SB_HEADS * SB_HEAD_DIM == SEG
    cs, sn = _rotary_tables(seq)
    ret_tabs = _retention_tables()
    b_tab = jnp.broadcast_to(sgu_b[:, :, :, None], sgu_b.shape + (LANES,))
    post_weights = (p_ret, p_sb, p_sgu, w_out, w_up, w_down)
    x2 = x.reshape(t, d)
    w_proj = w_in[0].astype(BF16)
    for l in range(DEPTH):
        proj, ret, sb, weights = _in_proj(x2, w_proj, cs, sn, sgu_ln_g, sgu_ln_b, ret_tabs,
                                          ret_gn_g, ret_gn_b, post_weights, l, tm, seq)
        next_proj = (w_in,) if l + 1 < DEPTH else ()
        x2, cast = _post(x2, ret, sb, proj, sgu_w, b_tab, weights, ln1_g, ln1_b, ln2_g, ln2_b,
                         next_proj, l, tm)
        if cast:
            w_proj = cast[0]
    return x2.reshape(batch, seq, d)
```

```python
import functools

import jax
import jax.numpy as jnp
import numpy as np
from jax import lax
from jax.experimental import pallas as pl
from jax.experimental.pallas import tpu as pltpu

F32 = jnp.float32
BF16 = jnp.bfloat16

D_MODEL = 1024
DEPTH = 2
CHUNK = 128
RET_HEADS = 4
RET_HEAD_DIM = 128
SB_HEADS = 8
SB_HEAD_DIM = 64
SGU_GROUPS = 4
SEG = 512
D_FF = 4 * D_MODEL
N_IN = 9 * SEG + 3 * D_MODEL
ROPE_BASE = 10000.0
LN_EPS = 1e-5
DEEPNORM_ALPHA = (2 * DEPTH) ** 0.25

SEG_RQ, SEG_RK, SEG_RV, SEG_RG, SEG_SQ, SEG_SK, SEG_SV, SEG_GU, SEG_GV = range(9)
SEG_GATE0 = 9
N_SEG = N_IN // SEG
PROJ_SEG0 = SEG_GU
N_PROJ = (N_SEG - PROJ_SEG0) * SEG

LANES = 128
BF16_SUBLANES = 16
VMEM_LIMIT = 60 * 1024 * 1024


def _cparams(n_axes):
    return pltpu.CompilerParams(
        dimension_semantics=("arbitrary",) * n_axes,
        vmem_limit_bytes=VMEM_LIMIT)


def _resident(shape):
    zeros = (0,) * len(shape)
    return pl.BlockSpec(shape, lambda *_: zeros, pipeline_mode=pl.Buffered(1))


def _of_layer(shape, l):
    zeros = (0,) * len(shape)
    return pl.BlockSpec((None,) + tuple(shape), lambda *_: (l,) + zeros,
                        pipeline_mode=pl.Buffered(1))


def _cast_rider(params, l, n_steps):
    in_specs, out_specs, out_shapes = [], [], []
    for p in params:
        _, r, c = p.shape
        assert r % (n_steps * BF16_SUBLANES) == 0
        rb = r // n_steps
        in_specs.append(pl.BlockSpec((None, rb, c), lambda i: (l, i, 0)))
        out_specs.append(pl.BlockSpec((rb, c), lambda i: (i, 0)))
        out_shapes.append(jax.ShapeDtypeStruct((r, c), BF16))
    return in_specs, out_specs, out_shapes


def _run_casts(src_refs, dst_refs):
    for src, dst in zip(src_refs, dst_refs):
        dst[...] = src[...].astype(BF16)


def _layer_norm(x, g, b):
    mu = jnp.mean(x, axis=-1, keepdims=True)
    xc = x - mu
    var = jnp.mean(xc * xc, axis=-1, keepdims=True)
    return xc * lax.rsqrt(var + LN_EPS) * g + b


SB_Q = 256
SB_K = 256
SB_GROUPS = 2
LOG2E = 1.4426950408889634
SB_UNDERFLOW = 152.0
SB_MASKED = -1e30
SB_CLAMP = 126.0


class _SbUnit:
    def __init__(self, q_tile, k_ref, v_ref, acc_ref, run_ref, col0, qi, tri, causal, low):
        self.groups = range(SB_GROUPS)
        self.k_ref, self.v_ref, self.qi, self.tri, self.causal, self.low = (
            k_ref, v_ref, qi, tri, causal, low)
        self.acc_ref, self.run_ref = acc_ref, run_ref
        self.lanes = [slice(col0 + g * LANES, col0 + (g + 1) * LANES) for g in self.groups]
        self.qs = []
        for g in self.groups:
            q = q_tile[:, g * LANES:(g + 1) * LANES]
            zero = jnp.zeros_like(q)
            self.qs.append(jnp.concatenate(
                [jnp.where(low, q, zero), jnp.where(low, zero, q)], axis=0))

    @staticmethod
    def key_rows(kj):
        return pl.ds(pl.multiple_of(kj * SB_K, SB_K), SB_K)

    def scores(self, rows, g):
        return lax.dot_general(self.qs[g], self.k_ref[rows, self.lanes[g]],
                               (((1,), (1,)), ((), ())), preferred_element_type=F32)

    def log_terms(self, z):
        sp = jnp.maximum(jnp.log(1.0 + jnp.exp2(jnp.minimum(z, SB_CLAMP))) * LOG2E, z)
        sp16 = sp.astype(BF16)
        later = jnp.dot(sp16, self.tri, preferred_element_type=F32)
        return z - sp, later, later[:, 0:1] + sp16[:, 0:1].astype(F32)

    def weighted(self, terms, run, rows, g):
        log_sig, later, _ = terms
        a = jnp.exp2(log_sig - later - run)
        return jnp.dot(a.astype(BF16), self.v_ref[rows, self.lanes[g]], preferred_element_type=F32)

    def stage_scores(self):
        self.rows_d = self.key_rows(self.qi)
        self.rows_p = self.key_rows(jnp.maximum(self.qi - 1, 0))
        self.z_d = [jnp.where(self.causal, self.scores(self.rows_d, g), SB_MASKED)
                    for g in self.groups]
        self.z_p = [self.scores(self.rows_p, g) for g in self.groups]

    def stage_sums(self):
        self.terms_d = [self.log_terms(z) for z in self.z_d]
        self.terms_p = [self.log_terms(z) for z in self.z_p]

    def stage_values(self):
        no_prev = jnp.where(self.qi == 0, -SB_MASKED, 0.0)
        run_p = [self.terms_d[g][2] + no_prev for g in self.groups]
        run = [run_p[g] + self.terms_p[g][2] for g in self.groups]
        for g in self.groups:
            self.acc_ref[g] = (self.weighted(self.terms_d[g], 0.0, self.rows_d, g)
                               + self.weighted(self.terms_p[g], run_p[g], self.rows_p, g))
            self.run_ref[g] = run[g]
        self.lowest_run = self.lowest(run)

    @staticmethod
    def lowest(run):
        return functools.reduce(jnp.minimum, [jnp.min(r) for r in run])

    def finish(self):
        qi = self.qi

        def live(c):
            return jnp.logical_and(c[0] < qi, c[1] <= SB_UNDERFLOW)

        def step(c):
            i = c[0]
            rows = self.key_rows(qi - 1 - i)
            terms = [self.log_terms(self.scores(rows, g)) for g in self.groups]
            run = [self.run_ref[g] for g in self.groups]
            for g in self.groups:
                self.acc_ref[g] += self.weighted(terms[g], run[g], rows, g)
                run[g] = run[g] + terms[g][2]
                self.run_ref[g] = run[g]
            return i + 1, self.lowest(run)

        @pl.when(live((jnp.int32(1), self.lowest_run)))
        def _():
            lax.while_loop(live, step, (jnp.int32(1), self.lowest_run))

        return jnp.concatenate(
            [jnp.where(self.low, self.acc_ref[g, :SB_Q], self.acc_ref[g, SB_Q:]).astype(BF16)
             for g in self.groups], axis=1)


def _inproj_kernel(n_casts, steps_per_seq, l, x_ref, w_ref, cs_ref, sn_ref, lng_ref, lnb_ref,
                   dint_ref, qd_ref, kd_ref, cd_ref, gng_ref, gnb_ref, *refs):
    o_ref, ret_ref, sb_ref = refs[n_casts:n_casts + 3]
    state_ref, ksc_ref, vsc_ref, acc_ref, run_ref = refs[-5:]
    _run_casts(refs[:n_casts], refs[n_casts + 3:-5])
    step_in_seq = pl.program_id(0) % steps_per_seq

    @pl.when(step_in_seq == 0)
    def _():
        state_ref[...] = jnp.zeros_like(state_ref)

    tm = x_ref.shape[0]
    xb = x_ref[...].astype(BF16)
    cs = cs_ref[...]
    sn = sn_ref[...]
    heads = range(RET_HEADS)
    n_chunks = tm // CHUNK
    chains = [(c, h) for c in range(n_chunks) for h in heads]
    hcols = [slice(h * RET_HEAD_DIM, (h + 1) * RET_HEAD_DIM) for h in heads]
    crows = [slice(c * CHUNK, (c + 1) * CHUNK) for c in range(n_chunks)]
    nt = (((1,), (1,)), ((), ()))
    tn = (((0,), (0,)), ((), ()))

    def project(j):
        return jnp.dot(xb, w_ref[:, j * SEG:(j + 1) * SEG], preferred_element_type=F32)

    def rotary(acc, scale):
        out = []
        for h in heads:
            xh = acc[:, hcols[h]]
            r = xh * cs + pltpu.roll(xh, RET_HEAD_DIM // 2, axis=1) * sn
            out.append((r * scale).astype(BF16) if scale is not None else r.astype(BF16))
        return out

    def emit(j):
        acc = project(j)
        if j == SEG_GU:
            res = jax.nn.gelu(acc)
        elif j == SEG_GV:
            res = _layer_norm(jax.nn.gelu(acc), lng_ref[l:l + 1, :], lnb_ref[l:l + 1, :])
        else:
            res = acc
        o_ref[:, (j - PROJ_SEG0) * SEG:(j - PROJ_SEG0 + 1) * SEG] = res.astype(BF16)

    tile_rows = pl.ds(pl.multiple_of(step_in_seq * tm, tm), tm)
    sq = (project(SEG_SQ) * (SB_HEAD_DIM ** -0.5 * LOG2E)).astype(BF16)
    ksc_ref[tile_rows, :] = project(SEG_SK).astype(BF16)
    vsc_ref[tile_rows, :] = project(SEG_SV).astype(BF16)

    low = lax.broadcasted_iota(jnp.int32, (SB_Q, LANES), 1) < SB_HEAD_DIM
    tri = jnp.where(lax.broadcasted_iota(jnp.int32, (SB_K, SB_K), 0)
                    > lax.broadcasted_iota(jnp.int32, (SB_K, SB_K), 1), 1.0, 0.0).astype(BF16)
    causal = (lax.broadcasted_iota(jnp.int32, (2 * SB_Q, SB_K), 1)
              < (lax.broadcasted_iota(jnp.int32, (2 * SB_Q, SB_K), 0) & (SB_Q - 1)))
    unit_cols = SB_GROUPS * LANES
    units = []
    for qb in range(tm // SB_Q):
        for u in range(SEG // unit_cols):
            q_tile = sq[qb * SB_Q:(qb + 1) * SB_Q, u * unit_cols:(u + 1) * unit_cols]
            units.append((qb, u, _SbUnit(q_tile, ksc_ref, vsc_ref, acc_ref.at[len(units)],
                                         run_ref.at[len(units)], u * unit_cols,
                                         step_in_seq * (tm // SB_Q) + qb, tri, causal, low)))
    sb_stages = [stage for _, _, unit in units
                 for stage in (unit.stage_scores, unit.stage_sums, unit.stage_values)]

    def sb_advance(n):
        for _ in range(n):
            if sb_stages:
                sb_stages.pop(0)()

    sb_advance(1)
    rq = rotary(project(SEG_RQ), None)
    sb_advance(1)
    rk = rotary(project(SEG_RK), RET_HEAD_DIM ** -0.5)
    sb_advance(1)
    rv = project(SEG_RV).astype(BF16)
    sb_advance(1)
    rg = jax.nn.silu(project(SEG_RG)).astype(BF16)
    q = {(c, h): rq[h][crows[c]] for c, h in chains}
    k = {(c, h): rk[h][crows[c]] for c, h in chains}
    v = {(c, h): rv[crows[c], hcols[h]] for c, h in chains}
    s = {(c, h): lax.dot_general(q[c, h], k[c, h], nt, preferred_element_type=F32) * dint_ref[h]
         for c, h in chains}
    kdec = {(c, h): (k[c, h].astype(F32) * kd_ref[h]).astype(BF16) for c, h in chains}
    kv = {(c, h): lax.dot_general(kdec[c, h], v[c, h], tn, preferred_element_type=F32)
          for c, h in chains}
    qdec = {(c, h): (q[c, h].astype(F32) * qd_ref[h]).astype(BF16) for c, h in chains}
    sb_advance(1)
    emit(SEG_GU)
    sb_advance(1)
    st = {(0, h): state_ref[h] for h in heads}
    for c in range(n_chunks):
        for h in heads:
            st[c + 1, h] = st[c, h] * cd_ref[h] + kv[c, h]
    for h in heads:
        state_ref[h] = st[n_chunks, h]
    ret = {(c, h): jnp.dot(jnp.concatenate([s[c, h].astype(BF16), qdec[c, h]], axis=1),
                           jnp.concatenate([v[c, h], st[c, h].astype(BF16)], axis=0),
                           preferred_element_type=F32)
           for c, h in chains}
    emit(SEG_GV)
    sb_advance(1)
    for c, h in chains:
        y = _layer_norm(ret[c, h], gng_ref[l:l + 1, hcols[h]], gnb_ref[l:l + 1, hcols[h]])
        ret_ref[crows[c], hcols[h]] = (rg[crows[c], hcols[h]].astype(F32) * y).astype(BF16)
    for j in range(SEG_GATE0, N_SEG):
        emit(j)
        sb_advance(1)
    sb_advance(len(sb_stages))
    for qb, u, unit in units:
        sb_ref[qb * SB_Q:(qb + 1) * SB_Q, u * unit_cols:(u + 1) * unit_cols] = unit.finish()


def _in_proj(x2, w, cs, sn, lng, lnb, ret_tabs, gng, gnb, to_cast, l, tm, seq):
    t = x2.shape[0]
    pos_blocks = seq // tm
    cast_in, cast_out, cast_shapes = _cast_rider(to_cast, l, t // tm)
    tab_spec = _resident((RET_HEADS, CHUNK, RET_HEAD_DIM))
    n_units = (tm // SB_Q) * (SEG // (SB_GROUPS * LANES))
    out = pl.pallas_call(
        functools.partial(_inproj_kernel, len(to_cast), pos_blocks, l),
        grid=(t // tm,),
        in_specs=[
            pl.BlockSpec((tm, D_MODEL), lambda i: (i, 0)),
            _resident((D_MODEL, N_IN)),
            pl.BlockSpec((tm, LANES), lambda i: (i % pos_blocks, 0)),
            pl.BlockSpec((tm, LANES), lambda i: (i % pos_blocks, 0)),
            _resident(lng.shape),
            _resident(lnb.shape),
            tab_spec, tab_spec, tab_spec, tab_spec,
            _resident(gng.shape),
            _resident(gnb.shape),
        ] + cast_in,
        out_specs=[pl.BlockSpec((tm, N_PROJ), lambda i: (i, 0)),
                   pl.BlockSpec((tm, SEG), lambda i: (i, 0)),
                   pl.BlockSpec((tm, SEG), lambda i: (i, 0))] + cast_out,
        out_shape=[jax.ShapeDtypeStruct((t, N_PROJ), BF16),
                   jax.ShapeDtypeStruct((t, SEG), BF16),
                   jax.ShapeDtypeStruct((t, SEG), BF16)] + cast_shapes,
        scratch_shapes=[pltpu.VMEM((RET_HEADS, RET_HEAD_DIM, RET_HEAD_DIM), F32),
                        pltpu.VMEM((seq, SEG), BF16), pltpu.VMEM((seq, SEG), BF16),
                        pltpu.VMEM((n_units, SB_GROUPS, 2 * SB_Q, LANES), F32),
                        pltpu.VMEM((n_units, SB_GROUPS, 2 * SB_Q, 1), F32)],
        compiler_params=_cparams(1),
        name="in_proj",
    )(x2, w, cs, sn, lng, lnb, *ret_tabs, gng, gnb, *to_cast)
    return out[0], out[1], out[2], out[3:]


FF_CHUNK = 1024
POST_PARTS = 2


def _post_kernel(n_casts, l, x_ref, ret_ref, sb_ref, gu_ref, gv_ref, g0, g1, g2, g3, g4, g5,
                 sw_ref, sbias_ref, pr_ref, ps_ref, pg_ref, wo_ref, ln1g_ref, ln1b_ref,
                 wu_ref, wd_ref, ln2g_ref, ln2b_ref, *refs):
    o_ref = refs[n_casts]
    sg_ref = refs[-1]
    _run_casts(refs[:n_casts], refs[n_casts + 1:-1])
    row = lax.broadcasted_iota(jnp.int32, (CHUNK, CHUNK), 0)
    col = lax.broadcasted_iota(jnp.int32, (CHUNK, CHUNK), 1)
    causal = row >= col
    for g in range(SGU_GROUPS):
        wm = jnp.where(causal, sw_ref[g], 0.0).astype(BF16)
        bias = sbias_ref[g]
        gc = slice(g * LANES, (g + 1) * LANES)
        for c in range(x_ref.shape[0] // CHUNK):
            rc = slice(c * CHUNK, (c + 1) * CHUNK)
            sv = jnp.dot(wm, gv_ref[rc, gc], preferred_element_type=F32) + bias
            sg_ref[rc, gc] = (gu_ref[rc, gc].astype(F32) * sv).astype(BF16)

    gates = ((g0, g1), (g2, g3), (g4, g5))
    branches = ((ret_ref, pr_ref), (sb_ref, ps_ref), (sg_ref, pg_ref))
    parts = range(POST_PARTS)
    part_rows = x_ref.shape[0] // POST_PARTS
    rows = [slice(p * part_rows, (p + 1) * part_rows) for p in parts]
    merged = []
    for p in parts:
        halves = []
        for half in range(D_MODEL // SEG):
            cols = slice(half * SEG, (half + 1) * SEG)
            m = None
            for (a_ref, p_ref), gate in zip(branches, gates):
                term = jax.nn.sigmoid(gate[half][rows[p], :].astype(F32)) * jnp.dot(
                    a_ref[rows[p], :], p_ref[:, cols], preferred_element_type=F32)
                m = term if m is None else m + term
            halves.append(m.astype(BF16))
        merged.append(halves)
    y = [jnp.dot(merged[p][0], wo_ref[:SEG, :], preferred_element_type=F32)
         + jnp.dot(merged[p][1], wo_ref[SEG:, :], preferred_element_type=F32) for p in parts]
    x1 = [_layer_norm(DEEPNORM_ALPHA * x_ref[rows[p], :] + y[p],
                      ln1g_ref[l:l + 1, :], ln1b_ref[l:l + 1, :]) for p in parts]

    xb = [x1[p].astype(BF16) for p in parts]
    acc = [None] * POST_PARTS
    for c in range(D_FF // FF_CHUNK):
        cols = slice(c * FF_CHUNK, (c + 1) * FF_CHUNK)
        h = [jnp.maximum(jnp.dot(xb[p], wu_ref[:, cols], preferred_element_type=F32), 0.0)
             for p in parts]
        for p in parts:
            part = jnp.dot((h[p] * h[p]).astype(BF16), wd_ref[cols, :], preferred_element_type=F32)
            acc[p] = part if acc[p] is None else acc[p] + part
    for p in parts:
        o_ref[rows[p], :] = _layer_norm(DEEPNORM_ALPHA * x1[p] + acc[p],
                                        ln2g_ref[l:l + 1, :], ln2b_ref[l:l + 1, :])


def _post(x2, ret, sb, proj, sgu_w, b_tab, weights, ln1g, ln1b, ln2g, ln2b, to_cast, l, tm):
    t = x2.shape[0]
    p_ret, p_sb, p_sgu, w_out, w_up, w_down = weights
    row_spec = lambda w: pl.BlockSpec((tm, w), lambda i: (i, 0))
    seg_spec = lambda seg: pl.BlockSpec((tm, SEG), lambda i: (i, seg - PROJ_SEG0))
    cast_in, cast_out, cast_shapes = _cast_rider(to_cast, l + 1, t // tm)
    out = pl.pallas_call(
        functools.partial(_post_kernel, len(to_cast), l),
        grid=(t // tm,),
        in_specs=[row_spec(D_MODEL), row_spec(SEG), row_spec(SEG),
                  seg_spec(SEG_GU), seg_spec(SEG_GV)]
        + [seg_spec(SEG_GATE0 + j) for j in range(6)]
        + [_of_layer((SGU_GROUPS, CHUNK, CHUNK), l), _of_layer((SGU_GROUPS, CHUNK, LANES), l)]
        + [_resident((SEG, D_MODEL))] * 3
        + [_resident((D_MODEL, D_MODEL)), _resident(ln1g.shape), _resident(ln1b.shape),
           _resident((D_MODEL, D_FF)), _resident((D_FF, D_MODEL)),
           _resident(ln2g.shape), _resident(ln2b.shape)]
        + cast_in,
        out_specs=[row_spec(D_MODEL)] + cast_out,
        out_shape=[jax.ShapeDtypeStruct((t, D_MODEL), F32)] + cast_shapes,
        scratch_shapes=[pltpu.VMEM((tm, SEG), BF16)],
        compiler_params=_cparams(1),
        name="post_mixer",
    )(x2, ret, sb, *([proj] * 8), sgu_w, b_tab, p_ret, p_sb, p_sgu, w_out, ln1g, ln1b,
      w_up, w_down, ln2g, ln2b, *to_cast)
    return out[0], out[1:]


def _rotary_tables(seq):
    half = RET_HEAD_DIM // 2
    inv_freq = ROPE_BASE ** (-np.arange(half, dtype=np.float64) / half)
    ang = np.arange(seq, dtype=np.float64)[:, None] * inv_freq[None, :]
    cos, sin = np.cos(ang), np.sin(ang)
    return (np.concatenate([cos, cos], axis=1).astype(np.float32),
            np.concatenate([-sin, sin], axis=1).astype(np.float32))


def _retention_tables():
    log_g = np.log(1.0 - 2.0 ** (-5.0 - np.arange(RET_HEADS, dtype=np.float64)))
    idx = np.arange(CHUNK, dtype=np.float64)
    diff = idx[:, None] - idx[None, :]
    dint = np.where(diff[None] >= 0, np.exp(log_g[:, None, None] * diff[None]), 0.0)
    k_decay = np.exp(log_g[:, None] * (CHUNK - 1 - idx)[None, :])
    q_decay = np.exp(log_g[:, None] * (idx + 1.0)[None, :])
    chunk_decay = np.exp(log_g * CHUNK)
    full = (RET_HEADS, CHUNK, RET_HEAD_DIM)
    return tuple(np.ascontiguousarray(a, dtype=np.float32) for a in (
        dint,
        np.broadcast_to(q_decay[:, :, None], full),
        np.broadcast_to(k_decay[:, :, None], full),
        np.broadcast_to(chunk_decay[:, None, None], full)))


def kernel(x, w_in, ret_gn_g, ret_gn_b, sgu_ln_g, sgu_ln_b, sgu_w, sgu_b, p_ret, p_sb, p_sgu,
           w_out, ln1_g, ln1_b, w_up, w_down, ln2_g, ln2_b):
    batch, seq, d = x.shape
    assert d == D_MODEL and seq % CHUNK == 0
    t = batch * seq
    tm = 512
    assert seq % tm == 0 and tm % SB_Q == 0 and SB_Q == SB_K and SB_HEADS * SB_HEAD_DIM == SEG
    cs, sn = _rotary_tables(seq)
    ret_tabs = _retention_tables()
    b_tab = jnp.broadcast_to(sgu_b[:, :, :, None], sgu_b.shape + (LANES,))
    post_weights = (p_ret, p_sb, p_sgu, w_out, w_up, w_down)
    x2 = x.reshape(t, d)
    w_proj = w_in[0].astype(BF16)
    for l in range(DEPTH):
        proj, ret, sb, weights = _in_proj(x2, w_proj, cs, sn, sgu_ln_g, sgu_ln_b, ret_tabs,
                                          ret_gn_g, ret_gn_b, post_weights, l, tm, seq)
        next_proj = (w_in,) if l + 1 < DEPTH else ()
        x2, cast = _post(x2, ret, sb, proj, sgu_w, b_tab, weights, ln1_g, ln1_b, ln2_g, ln2_b,
                         next_proj, l, tm)
        if cast:
            w_proj = cast[0]
    return x2.reshape(batch, seq, d)
```

```python
import functools

import jax
import jax.numpy as jnp
import numpy as np
from jax import lax
from jax.experimental import pallas as pl
from jax.experimental.pallas import tpu as pltpu

F32 = jnp.float32
BF16 = jnp.bfloat16

D_MODEL = 1024
DEPTH = 2
CHUNK = 128
RET_HEADS = 4
RET_HEAD_DIM = 128
SB_HEADS = 8
SB_HEAD_DIM = 64
SGU_GROUPS = 4
SEG = 512
D_FF = 4 * D_MODEL
N_IN = 9 * SEG + 3 * D_MODEL
ROPE_BASE = 10000.0
LN_EPS = 1e-5
DEEPNORM_ALPHA = (2 * DEPTH) ** 0.25

SEG_RQ, SEG_RK, SEG_RV, SEG_RG, SEG_SQ, SEG_SK, SEG_SV, SEG_GU, SEG_GV = range(9)
SEG_GATE0 = 9
N_SEG = N_IN // SEG
PROJ_SEG0 = SEG_GU
N_PROJ = (N_SEG - PROJ_SEG0) * SEG

LANES = 128
BF16_SUBLANES = 16
VMEM_LIMIT = 60 * 1024 * 1024


def _cparams(n_axes):
    return pltpu.CompilerParams(
        dimension_semantics=("arbitrary",) * n_axes,
        vmem_limit_bytes=VMEM_LIMIT)


def _resident(shape):
    zeros = (0,) * len(shape)
    return pl.BlockSpec(shape, lambda *_: zeros, pipeline_mode=pl.Buffered(1))


def _of_layer(shape, l):
    zeros = (0,) * len(shape)
    return pl.BlockSpec((None,) + tuple(shape), lambda *_: (l,) + zeros,
                        pipeline_mode=pl.Buffered(1))


def _cast_rider(params, l, n_steps):
    in_specs, out_specs, out_shapes = [], [], []
    for p in params:
        _, r, c = p.shape
        assert r % (n_steps * BF16_SUBLANES) == 0
        rb = r // n_steps
        in_specs.append(pl.BlockSpec((None, rb, c), lambda i: (l, i, 0)))
        out_specs.append(pl.BlockSpec((rb, c), lambda i: (i, 0)))
        out_shapes.append(jax.ShapeDtypeStruct((r, c), BF16))
    return in_specs, out_specs, out_shapes


def _run_casts(src_refs, dst_refs):
    for src, dst in zip(src_refs, dst_refs):
        dst[...] = src[...].astype(BF16)


def _layer_norm(x, g, b):
    mu = jnp.mean(x, axis=-1, keepdims=True)
    xc = x - mu
    var = jnp.mean(xc * xc, axis=-1, keepdims=True)
    return xc * lax.rsqrt(var + LN_EPS) * g + b


SB_Q = 256
SB_K = 256
SB_GROUPS = 2
LOG2E = 1.4426950408889634
SB_UNDERFLOW = 152.0
SB_MASKED = -1e30
SB_CLAMP = 126.0


class _SbUnit:
    def __init__(self, q_tile, k_ref, v_ref, acc_ref, run_ref, col0, qi, tri, causal, low):
        self.groups = range(SB_GROUPS)
        self.k_ref, self.v_ref, self.qi, self.tri, self.causal, self.low = (
            k_ref, v_ref, qi, tri, causal, low)
        self.acc_ref, self.run_ref = acc_ref, run_ref
        self.lanes = [slice(col0 + g * LANES, col0 + (g + 1) * LANES) for g in self.groups]
        self.qs = []
        for g in self.groups:
            q = q_tile[:, g * LANES:(g + 1) * LANES]
            zero = jnp.zeros_like(q)
            self.qs.append(jnp.concatenate(
                [jnp.where(low, q, zero), jnp.where(low, zero, q)], axis=0))

    @staticmethod
    def key_rows(kj):
        return pl.ds(pl.multiple_of(kj * SB_K, SB_K), SB_K)

    def scores(self, rows, g):
        return lax.dot_general(self.qs[g], self.k_ref[rows, self.lanes[g]],
                               (((1,), (1,)), ((), ())), preferred_element_type=F32)

    def log_terms(self, z):
        sp = jnp.maximum(jnp.log(1.0 + jnp.exp2(jnp.minimum(z, SB_CLAMP))) * LOG2E, z)
        sp16 = sp.astype(BF16)
        later = jnp.dot(sp16, self.tri, preferred_element_type=F32)
        return z - sp, later, later[:, 0:1] + sp16[:, 0:1].astype(F32)

    def weighted(self, terms, run, rows, g):
        log_sig, later, _ = terms
        a = jnp.exp2(log_sig - later - run)
        return jnp.dot(a.astype(BF16), self.v_ref[rows, self.lanes[g]], preferred_element_type=F32)

    def stage_scores(self):
        self.rows_d = self.key_rows(self.qi)
        self.rows_p = self.key_rows(jnp.maximum(self.qi - 1, 0))
        self.z_d = [jnp.where(self.causal, self.scores(self.rows_d, g), SB_MASKED)
                    for g in self.groups]
        self.z_p = [self.scores(self.rows_p, g) for g in self.groups]

    def stage_sums(self):
        self.terms_d = [self.log_terms(z) for z in self.z_d]
        self.terms_p = [self.log_terms(z) for z in self.z_p]

    def stage_values(self):
        no_prev = jnp.where(self.qi == 0, -SB_MASKED, 0.0)
        run_p = [self.terms_d[g][2] + no_prev for g in self.groups]
        run = [run_p[g] + self.terms_p[g][2] for g in self.groups]
        for g in self.groups:
            self.acc_ref[g] = (self.weighted(self.terms_d[g], 0.0, self.rows_d, g)
                               + self.weighted(self.terms_p[g], run_p[g], self.rows_p, g))
            self.run_ref[g] = run[g]
        self.lowest_run = self.lowest(run)

    @staticmethod
    def lowest(run):
        return functools.reduce(jnp.minimum, [jnp.min(r) for r in run])

    def finish(self):
        qi = self.qi

        def live(c):
            return jnp.logical_and(c[0] < qi, c[1] <= SB_UNDERFLOW)

        def step(c):
            i = c[0]
            rows = self.key_rows(qi - 1 - i)
            terms = [self.log_terms(self.scores(rows, g)) for g in self.groups]
            run = [self.run_ref[g] for g in self.groups]
            for g in self.groups:
                self.acc_ref[g] += self.weighted(terms[g], run[g], rows, g)
                run[g] = run[g] + terms[g][2]
                self.run_ref[g] = run[g]
            return i + 1, self.lowest(run)

        @pl.when(live((jnp.int32(1), self.lowest_run)))
        def _():
            lax.while_loop(live, step, (jnp.int32(1), self.lowest_run))

        return jnp.concatenate(
            [jnp.where(self.low, self.acc_ref[g, :SB_Q], self.acc_ref[g, SB_Q:]).astype(BF16)
             for g in self.groups], axis=1)


def _inproj_kernel(n_casts, steps_per_seq, l, x_ref, w_ref, cs_ref, sn_ref, lng_ref, lnb_ref,
                   dint_ref, qd_ref, kd_ref, cd_ref, gng_ref, gnb_ref, *refs):
    o_ref, ret_ref, sb_ref = refs[n_casts:n_casts + 3]
    state_ref, ksc_ref, vsc_ref, acc_ref, run_ref = refs[-5:]
    _run_casts(refs[:n_casts], refs[n_casts + 3:-5])
    step_in_seq = pl.program_id(0) % steps_per_seq

    @pl.when(step_in_seq == 0)
    def _():
        state_ref[...] = jnp.zeros_like(state_ref)

    tm = x_ref.shape[0]
    xb = x_ref[...].astype(BF16)
    cs = cs_ref[...]
    sn = sn_ref[...]
    heads = range(RET_HEADS)
    n_chunks = tm // CHUNK
    chains = [(c, h) for c in range(n_chunks) for h in heads]
    hcols = [slice(h * RET_HEAD_DIM, (h + 1) * RET_HEAD_DIM) for h in heads]
    crows = [slice(c * CHUNK, (c + 1) * CHUNK) for c in range(n_chunks)]
    nt = (((1,), (1,)), ((), ()))
    tn = (((0,), (0,)), ((), ()))

    def project(j):
        return jnp.dot(xb, w_ref[:, j * SEG:(j + 1) * SEG], preferred_element_type=F32)

    def rotary(acc, scale):
        out = []
        for h in heads:
            xh = acc[:, hcols[h]]
            r = xh * cs + pltpu.roll(xh, RET_HEAD_DIM // 2, axis=1) * sn
            out.append((r * scale).astype(BF16) if scale is not None else r.astype(BF16))
        return out

    def emit(j):
        acc = project(j)
        if j == SEG_GU:
            res = jax.nn.gelu(acc)
        elif j == SEG_GV:
            res = _layer_norm(jax.nn.gelu(acc), lng_ref[l:l + 1, :], lnb_ref[l:l + 1, :])
        else:
            res = acc
        o_ref[:, (j - PROJ_SEG0) * SEG:(j - PROJ_SEG0 + 1) * SEG] = res.astype(BF16)

    tile_rows = pl.ds(pl.multiple_of(step_in_seq * tm, tm), tm)
    sq = (project(SEG_SQ) * (SB_HEAD_DIM ** -0.5 * LOG2E)).astype(BF16)
    ksc_ref[tile_rows, :] = project(SEG_SK).astype(BF16)
    vsc_ref[tile_rows, :] = project(SEG_SV).astype(BF16)

    low = lax.broadcasted_iota(jnp.int32, (SB_Q, LANES), 1) < SB_HEAD_DIM
    tri = jnp.where(lax.broadcasted_iota(jnp.int32, (SB_K, SB_K), 0)
                    > lax.broadcasted_iota(jnp.int32, (SB_K, SB_K), 1), 1.0, 0.0).astype(BF16)
    causal = (lax.broadcasted_iota(jnp.int32, (2 * SB_Q, SB_K), 1)
              < (lax.broadcasted_iota(jnp.int32, (2 * SB_Q, SB_K), 0) & (SB_Q - 1)))
    unit_cols = SB_GROUPS * LANES
    units = []
    for qb in range(tm // SB_Q):
        for u in range(SEG // unit_cols):
            q_tile = sq[qb * SB_Q:(qb + 1) * SB_Q, u * unit_cols:(u + 1) * unit_cols]
            units.append((qb, u, _SbUnit(q_tile, ksc_ref, vsc_ref, acc_ref.at[len(units)],
                                         run_ref.at[len(units)], u * unit_cols,
                                         step_in_seq * (tm // SB_Q) + qb, tri, causal, low)))
    sb_stages = [stage for _, _, unit in units
                 for stage in (unit.stage_scores, unit.stage_sums, unit.stage_values)]

    def sb_advance(n):
        for _ in range(n):
            if sb_stages:
                sb_stages.pop(0)()

    sb_advance(1)
    rq = rotary(project(SEG_RQ), None)
    sb_advance(1)
    rk = rotary(project(SEG_RK), RET_HEAD_DIM ** -0.5)
    sb_advance(1)
    rv = project(SEG_RV).astype(BF16)
    sb_advance(1)
    rg = jax.nn.silu(project(SEG_RG)).astype(BF16)
    q = {(c, h): rq[h][crows[c]] for c, h in chains}
    k = {(c, h): rk[h][crows[c]] for c, h in chains}
    v = {(c, h): rv[crows[c], hcols[h]] for c, h in chains}
    s = {(c, h): lax.dot_general(q[c, h], k[c, h], nt, preferred_element_type=F32) * dint_ref[h]
         for c, h in chains}
    kdec = {(c, h): (k[c, h].astype(F32) * kd_ref[h]).astype(BF16) for c, h in chains}
    kv = {(c, h): lax.dot_general(kdec[c, h], v[c, h], tn, preferred_element_type=F32)
          for c, h in chains}
    qdec = {(c, h): (q[c, h].astype(F32) * qd_ref[h]).astype(BF16) for c, h in chains}
    sb_advance(1)
    emit(SEG_GU)
    sb_advance(1)
    st = {(0, h): state_ref[h] for h in heads}
    for c in range(n_chunks):
        for h in heads:
            st[c + 1, h] = st[c, h] * cd_ref[h] + kv[c, h]
    for h in heads:
        state_ref[h] = st[n_chunks, h]
    ret = {(c, h): jnp.dot(jnp.concatenate([s[c, h].astype(BF16), qdec[c, h]], axis=1),
                           jnp.concatenate([v[c, h], st[c, h].astype(BF16)], axis=0),
                           preferred_element_type=F32)
           for c, h in chains}
    emit(SEG_GV)
    sb_advance(1)
    for c, h in chains:
        y = _layer_norm(ret[c, h], gng_ref[l:l + 1, hcols[h]], gnb_ref[l:l + 1, hcols[h]])
        ret_ref[crows[c], hcols[h]] = (rg[crows[c], hcols[h]].astype(F32) * y).astype(BF16)
    for j in range(SEG_GATE0, N_SEG):
        emit(j)
        sb_advance(1)
    sb_advance(len(sb_stages))
    for qb, u, unit in units:
        sb_ref[qb * SB_Q:(qb + 1) * SB_Q, u * unit_cols:(u + 1) * unit_cols] = unit.finish()


def _in_proj(x2, w, cs, sn, lng, lnb, ret_tabs, gng, gnb, to_cast, l, tm, seq):
    t = x2.shape[0]
    pos_blocks = seq // tm
    cast_in, cast_out, cast_shapes = _cast_rider(to_cast, l, t // tm)
    tab_spec = _resident((RET_HEADS, CHUNK, RET_HEAD_DIM))
    n_units = (tm // SB_Q) * (SEG // (SB_GROUPS * LANES))
    out = pl.pallas_call(
        functools.partial(_inproj_kernel, len(to_cast), pos_blocks, l),
        grid=(t // tm,),
        in_specs=[
            pl.BlockSpec((tm, D_MODEL), lambda i: (i, 0)),
            _resident((D_MODEL, N_IN)),
            pl.BlockSpec((tm, LANES), lambda i: (i % pos_blocks, 0)),
            pl.BlockSpec((tm, LANES), lambda i: (i % pos_blocks, 0)),
            _resident(lng.shape),
            _resident(lnb.shape),
            tab_spec, tab_spec, tab_spec, tab_spec,
            _resident(gng.shape),
            _resident(gnb.shape),
        ] + cast_in,
        out_specs=[pl.BlockSpec((tm, N_PROJ), lambda i: (i, 0)),
                   pl.BlockSpec((tm, SEG), lambda i: (i, 0)),
                   pl.BlockSpec((tm, SEG), lambda i: (i, 0))] + cast_out,
        out_shape=[jax.ShapeDtypeStruct((t, N_PROJ), BF16),
                   jax.ShapeDtypeStruct((t, SEG), BF16),
                   jax.ShapeDtypeStruct((t, SEG), BF16)] + cast_shapes,
        scratch_shapes=[pltpu.VMEM((RET_HEADS, RET_HEAD_DIM, RET_HEAD_DIM), F32),
                        pltpu.VMEM((seq, SEG), BF16), pltpu.VMEM((seq, SEG), BF16),
                        pltpu.VMEM((n_units, SB_GROUPS, 2 * SB_Q, LANES), F32),
                        pltpu.VMEM((n_units, SB_GROUPS, 2 * SB_Q, 1), F32)],
        compiler_params=_cparams(1),
        name="in_proj",
    )(x2, w, cs, sn, lng, lnb, *ret_tabs, gng, gnb, *to_cast)
    return out[0], out[1], out[2], out[3:]


FF_CHUNK = 1024
POST_PARTS = 2


def _post_kernel(n_casts, l, x_ref, ret_ref, sb_ref, gu_ref, gv_ref, g0, g1, g2, g3, g4, g5,
                 sw_ref, sbias_ref, pr_ref, ps_ref, pg_ref, wo_ref, ln1g_ref, ln1b_ref,
                 wu_ref, wd_ref, ln2g_ref, ln2b_ref, *refs):
    o_ref = refs[n_casts]
    sg_ref = refs[-1]
    _run_casts(refs[:n_casts], refs[n_casts + 1:-1])
    row = lax.broadcasted_iota(jnp.int32, (CHUNK, CHUNK), 0)
    col = lax.broadcasted_iota(jnp.int32, (CHUNK, CHUNK), 1)
    causal = row >= col
    for g in range(SGU_GROUPS):
        wm = jnp.where(causal, sw_ref[g], 0.0).astype(BF16)
        bias = sbias_ref[g]
        gc = slice(g * LANES, (g + 1) * LANES)
        for c in range(x_ref.shape[0] // CHUNK):
            rc = slice(c * CHUNK, (c + 1) * CHUNK)
            sv = jnp.dot(wm, gv_ref[rc, gc], preferred_element_type=F32) + bias
            sg_ref[rc, gc] = (gu_ref[rc, gc].astype(F32) * sv).astype(BF16)

    gates = ((g0, g1), (g2, g3), (g4, g5))
    branches = ((ret_ref, pr_ref), (sb_ref, ps_ref), (sg_ref, pg_ref))
    parts = range(POST_PARTS)
    part_rows = x_ref.shape[0] // POST_PARTS
    rows = [slice(p * part_rows, (p + 1) * part_rows) for p in parts]
    merged = []
    for p in parts:
        halves = []
        for half in range(D_MODEL // SEG):
            cols = slice(half * SEG, (half + 1) * SEG)
            m = None
            for (a_ref, p_ref), gate in zip(branches, gates):
                term = jax.nn.sigmoid(gate[half][rows[p], :].astype(F32)) * jnp.dot(
                    a_ref[rows[p], :], p_ref[:, cols], preferred_element_type=F32)
                m = term if m is None else m + term
            halves.append(m.astype(BF16))
        merged.append(halves)
    y = [jnp.dot(merged[p][0], wo_ref[:SEG, :], preferred_element_type=F32)
         + jnp.dot(merged[p][1], wo_ref[SEG:, :], preferred_element_type=F32) for p in parts]
    x1 = [_layer_norm(DEEPNORM_ALPHA * x_ref[rows[p], :] + y[p],
                      ln1g_ref[l:l + 1, :], ln1b_ref[l:l + 1, :]) for p in parts]

    xb = [x1[p].astype(BF16) for p in parts]
    acc = [None] * POST_PARTS
    for c in range(D_FF // FF_CHUNK):
        cols = slice(c * FF_CHUNK, (c + 1) * FF_CHUNK)
        h = [jnp.maximum(jnp.dot(xb[p], wu_ref[:, cols], preferred_element_type=F32), 0.0)
             for p in parts]
        for p in parts:
            part = jnp.dot((h[p] * h[p]).astype(BF16), wd_ref[cols, :], preferred_element_type=F32)
            acc[p] = part if acc[p] is None else acc[p] + part
    for p in parts:
        o_ref[rows[p], :] = _layer_norm(DEEPNORM_ALPHA * x1[p] + acc[p],
                                        ln2g_ref[l:l + 1, :], ln2b_ref[l:l + 1, :])


def _post(x2, ret, sb, proj, sgu_w, b_tab, weights, ln1g, ln1b, ln2g, ln2b, to_cast, l, tm):
    t = x2.shape[0]
    p_ret, p_sb, p_sgu, w_out, w_up, w_down = weights
    row_spec = lambda w: pl.BlockSpec((tm, w), lambda i: (i, 0))
    seg_spec = lambda seg: pl.BlockSpec((tm, SEG), lambda i: (i, seg - PROJ_SEG0))
    cast_in, cast_out, cast_shapes = _cast_rider(to_cast, l + 1, t // tm)
    in_specs = ([row_spec(D_MODEL), row_spec(SEG), row_spec(SEG),
                 seg_spec(SEG_GU), seg_spec(SEG_GV)]
                + [seg_spec(SEG_GATE0 + j) for j in range(6)]
                + [_of_layer((SGU_GROUPS, CHUNK, CHUNK), l), _of_layer((SGU_GROUPS, CHUNK, LANES), l)]
                + [_resident((SEG, D_MODEL))] * 3
                + [_resident((D_MODEL, D_MODEL)), _resident(ln1g.shape), _resident(ln1b.shape),
                   _resident((D_MODEL, D_FF)), _resident((D_FF, D_MODEL)),
                   _resident(ln2g.shape), _resident(ln2b.shape)]
                + cast_in)
    out_specs = [row_spec(D_MODEL)] + cast_out
    operands = (x2, ret, sb, *([proj] * 8), sgu_w, b_tab, p_ret, p_sb, p_sgu, w_out, ln1g, ln1b,
                w_up, w_down, ln2g, ln2b, *to_cast)

    def outer(*refs):
        sg_ref = refs[-1]
        body = lambda *blocks: _post_kernel(len(to_cast), l, *blocks, sg_ref)
        pltpu.emit_pipeline(body, grid=(t // tm,), in_specs=in_specs, out_specs=out_specs)(
            *refs[:-1])

    out = pl.pallas_call(
        outer,
        in_specs=[pl.BlockSpec(memory_space=pl.ANY)] * len(operands),
        out_specs=[pl.BlockSpec(memory_space=pl.ANY)] * (1 + len(to_cast)),
        out_shape=[jax.ShapeDtypeStruct((t, D_MODEL), F32)] + cast_shapes,
        scratch_shapes=[pltpu.VMEM((tm, SEG), BF16)],
        compiler_params=pltpu.CompilerParams(vmem_limit_bytes=VMEM_LIMIT),
        name="post_mixer",
    )(*operands)
    return out[0], out[1:]


def _rotary_tables(seq):
    half = RET_HEAD_DIM // 2
    inv_freq = ROPE_BASE ** (-np.arange(half, dtype=np.float64) / half)
    ang = np.arange(seq, dtype=np.float64)[:, None] * inv_freq[None, :]
    cos, sin = np.cos(ang), np.sin(ang)
    return (np.concatenate([cos, cos], axis=1).astype(np.float32),
            np.concatenate([-sin, sin], axis=1).astype(np.float32))


def _retention_tables():
    log_g = np.log(1.0 - 2.0 ** (-5.0 - np.arange(RET_HEADS, dtype=np.float64)))
    idx = np.arange(CHUNK, dtype=np.float64)
    diff = idx[:, None] - idx[None, :]
    dint = np.where(diff[None] >= 0, np.exp(log_g[:, None, None] * diff[None]), 0.0)
    k_decay = np.exp(log_g[:, None] * (CHUNK - 1 - idx)[None, :])
    q_decay = np.exp(log_g[:, None] * (idx + 1.0)[None, :])
    chunk_decay = np.exp(log_g * CHUNK)
    full = (RET_HEADS, CHUNK, RET_HEAD_DIM)
    return tuple(np.ascontiguousarray(a, dtype=np.float32) for a in (
        dint,
        np.broadcast_to(q_decay[:, :, None], full),
        np.broadcast_to(k_decay[:, :, None], full),
        np.broadcast_to(chunk_decay[:, None, None], full)))


def kernel(x, w_in, ret_gn_g, ret_gn_b, sgu_ln_g, sgu_ln_b, sgu_w, sgu_b, p_ret, p_sb, p_sgu,
           w_out, ln1_g, ln1_b, w_up, w_down, ln2_g, ln2_b):
    batch, seq, d = x.shape
    assert d == D_MODEL and seq % CHUNK == 0
    t = batch * seq
    tm = 512
    assert seq % tm == 0 and tm % SB_Q == 0 and SB_Q == SB_K and SB_HEADS * SB_HEAD_DIM == SEG
    cs, sn = _rotary_tables(seq)
    ret_tabs = _retention_tables()
    b_tab = jnp.broadcast_to(sgu_b[:, :, :, None], sgu_b.shape + (LANES,))
    post_weights = (p_ret, p_sb, p_sgu, w_out, w_up, w_down)
    x2 = x.reshape(t, d)
    w_proj = w_in[0].astype(BF16)
    for l in range(DEPTH):
        proj, ret, sb, weights = _in_proj(x2, w_proj, cs, sn, sgu_ln_g, sgu_ln_b, ret_tabs,
                                          ret_gn_g, ret_gn_b, post_weights, l, tm, seq)
        next_proj = (w_in,) if l + 1 < DEPTH else ()
        x2, cast = _post(x2, ret, sb, proj, sgu_w, b_tab, weights, ln1_g, ln1_b, ln2_g, ln2_b,
                         next_proj, l, tm)
        if cast:
            w_proj = cast[0]
    return x2.reshape(batch, seq, d)
```
